```python
import jax
import jax.numpy as jnp
from jax import lax
import numpy as np

D_MODEL = 1024
BATCH = 8
SEQ = 4096
DEPTH = 2

CONV_WIDTH = D_MODEL
CONV_TAPS = 3
RWKV_HEAD_DIM = 64
RWKV_WIDTH = D_MODEL
RWKV_HEADS = RWKV_WIDTH // RWKV_HEAD_DIM
RWKV_DECAY_LORA = max(32, int(round(1.8 * D_MODEL ** 0.5 / 32)) * 32)
RWKV_AAA_LORA = max(32, int(round(1.8 * D_MODEL ** 0.5 / 32)) * 32)
RWKV_MV_LORA = max(32, int(round(1.3 * D_MODEL ** 0.5 / 32)) * 32)
RWKV_GATE_LORA = max(32, int(round(0.6 * D_MODEL ** 0.8 / 32)) * 32)
RWKV_GN_EPS = 1e-5 * RWKV_HEAD_DIM
RWKV_COLS = 3 * RWKV_WIDTH + RWKV_DECAY_LORA + RWKV_AAA_LORA + RWKV_GATE_LORA
DILATED_GROUPS = ((128, 1), (512, 4), (2048, 16))
N_DIL_GROUPS = len(DILATED_GROUPS)
ATTN_HEADS_PER_GROUP = 8
ATTN_HEAD_DIM = 64
ATTN_WIDTH = N_DIL_GROUPS * ATTN_HEADS_PER_GROUP * ATTN_HEAD_DIM
ATTN_OUT_WIDTH = ATTN_HEADS_PER_GROUP * ATTN_HEAD_DIM
ATTN_BLOCK = 128
ROT_DIM = ATTN_HEAD_DIM // 4
ROPE_THETA = 500000.0
N_IN = 3 * D_MODEL + 3 * CONV_WIDTH + RWKV_COLS + 3 * ATTN_WIDTH
N_EXPERTS = 16
N_EXPERT_GROUPS = 4
EXPERTS_PER_GROUP = N_EXPERTS // N_EXPERT_GROUPS
TOP_K = 2
D_EXPERT = D_MODEL // 2
ALPHA = (2 * DEPTH) ** 0.25
BETA = (8 * DEPTH) ** -0.25
LN_EPS = 1e-5
NEG_INF = -1e30
MAX_POS_OFFSET = 1024

kernel_name = "hybrid_conv_rwkv7_dilated_attn_moe_deepnorm"


def split_cols(z, sizes):
    return jnp.split(z, np.cumsum(sizes)[:-1].tolist(), axis=-1)


def layer_norm(x, g, b):
    xf = x.astype(jnp.float32)
    mu = jnp.mean(xf, axis=-1, keepdims=True)
    var = jnp.mean(jnp.square(xf - mu), axis=-1, keepdims=True)
    return ((xf - mu) * lax.rsqrt(var + LN_EPS) * g + b).astype(x.dtype)


def causal_depthwise_conv(y, w):
    return lax.conv_general_dilated(
        y, w[:, None, :].astype(y.dtype), window_strides=(1,), padding=[(w.shape[0] - 1, 0)],
        dimension_numbers=('NWC', 'WIO', 'NWC'), feature_group_count=y.shape[-1])


def token_shift_mix(z, mu):
    prev = jnp.pad(z[:, :-1], ((0, 0), (1, 0), (0, 0)))
    return z + (prev - z) * mu


def rotary_tables(positions):
    inv_freq = ROPE_THETA ** (-jnp.arange(0, ROT_DIM, 2, dtype=jnp.float32) / ROT_DIM)
    ang = positions.astype(jnp.float32)[..., None] * inv_freq
    return jnp.cos(ang)[:, :, None, None, :], jnp.sin(ang)[:, :, None, None, :]


def apply_partial_rotary(t, cos, sin):
    half = ROT_DIM // 2
    t1 = t[..., :half].astype(jnp.float32)
    t2 = t[..., half:ROT_DIM].astype(jnp.float32)
    rot = jnp.concatenate([t1 * cos - t2 * sin, t2 * cos + t1 * sin], axis=-1).astype(t.dtype)
    return jnp.concatenate([rot, t[..., ROT_DIM:]], axis=-1)


def dilated_window_attention(q, k, v, window, dilation):
    b, s, h, dh = q.shape
    span = window // dilation
    assert span <= ATTN_BLOCK
    L = s // dilation
    nb = -(-L // ATTN_BLOCK)
    Lp = nb * ATTN_BLOCK
    Q = ATTN_BLOCK

    def to_sub(t):
        return t.reshape(b, L, dilation, h, dh).transpose(0, 2, 1, 3, 4)

    qs, ks, vs = to_sub(q), to_sub(k), to_sub(v)
    qb = jnp.pad(qs, ((0, 0), (0, 0), (0, Lp - L), (0, 0), (0, 0))).reshape(b, dilation, nb, Q, h, dh)

    def key_blocks(t):
        tp = jnp.pad(t, ((0, 0), (0, 0), (Q, Lp - L), (0, 0), (0, 0)))
        prev = tp[:, :, :Lp].reshape(b, dilation, nb, Q, h, dh)
        cur = tp[:, :, Q:].reshape(b, dilation, nb, Q, h, dh)
        return jnp.concatenate([prev, cur], axis=3)

    kb, vb = key_blocks(ks), key_blocks(vs)
    scores = jnp.einsum('brnqhc,brnkhc->brnhqk', qb, kb).astype(jnp.float32) * (ATTN_HEAD_DIM ** -0.5)
    qi = jnp.arange(Q)[:, None]
    kj = jnp.arange(2 * Q)[None, :]
    dist = Q + qi - kj
    key_idx = (jnp.arange(nb)[:, None, None] - 1) * Q + kj[None]
    valid = (dist >= 0) & (dist <= span) & (key_idx >= 0)
    scores = jnp.where(valid[:, None], scores, NEG_INF)
    m = jnp.max(scores, axis=-1, keepdims=True)
    p = jnp.exp(scores - m)
    den = jnp.sum(p, axis=-1, keepdims=True)
    out = jnp.einsum('brnhqk,brnkhc->brnqhc', (p / den).astype(v.dtype), vb)
    lse = (m + jnp.log(den))[..., 0]
    out = out.reshape(b, dilation, Lp, h, dh)[:, :, :L].transpose(0, 2, 1, 3, 4).reshape(b, s, h, dh)
    lse = lse.transpose(0, 1, 2, 4, 3).reshape(b, dilation, Lp, h)[:, :, :L]
    lse = lse.transpose(0, 2, 1, 3).reshape(b, s, h)
    return out, lse


def dilated_attention_branch(z, cos, sin):
    b, s, _ = z.shape
    q, k, v = [t.reshape(b, s, N_DIL_GROUPS, ATTN_HEADS_PER_GROUP, ATTN_HEAD_DIM)
               for t in jnp.split(z, 3, axis=-1)]
    q = apply_partial_rotary(q, cos, sin)
    k = apply_partial_rotary(k, cos, sin)
    outs, lses = [], []
    for g, (window, dilation) in enumerate(DILATED_GROUPS):
        o, l = dilated_window_attention(q[:, :, g], k[:, :, g], v[:, :, g], window, dilation)
        outs.append(o)
        lses.append(l)
    wts = jax.nn.softmax(jnp.stack(lses, axis=0), axis=0)
    y = jnp.sum(wts[..., None] * jnp.stack(outs, axis=0).astype(jnp.float32), axis=0)
    return y.reshape(b, s, ATTN_OUT_WIDTH).astype(z.dtype)


def wkv7_scan(r, w, k, v, a, b):
    bsz, _, h, n = r.shape

    def step(state, inp):
        rt, wt, kt, vt, at, bt = inp
        sa = jnp.einsum('bhvk,bhk->bhv', state, at)
        state = state * wt[:, :, None, :] + sa[..., None] * bt[:, :, None, :] + vt[..., None] * kt[:, :, None, :]
        return state, jnp.einsum('bhvk,bhk->bhv', state, rt)

    seq_first = tuple(jnp.moveaxis(t.astype(jnp.float32), 1, 0) for t in (r, w, k, v, a, b))
    _, y = lax.scan(step, jnp.zeros((bsz, h, n, n), jnp.float32), seq_first)
    return jnp.moveaxis(y, 0, 1)


def rwkv7_time_mix(u, z, v_first, params, vres):
    mu, w0, w2, a0, a2, g2, k_k, k_a, r_k, lnx_g, lnx_b = params
    b, s, _ = u.shape
    z = token_shift_mix(z, mu)
    r, k, v, xw, xa, xg = split_cols(z, (RWKV_WIDTH, RWKV_WIDTH, RWKV_WIDTH,
                                          RWKV_DECAY_LORA, RWKV_AAA_LORA, RWKV_GATE_LORA))
    log_w = -jax.nn.softplus(-(w0 + jnp.tanh(xw) @ w2)) - 0.5
    decay = jnp.exp(-jnp.exp(log_w.astype(jnp.float32)))
    if vres is None:
        v_first = v
    else:
        v0, v1, v2 = vres
        v = v + (v_first - v) * jax.nn.sigmoid(v0 + (u @ v1) @ v2)
    a = jax.nn.sigmoid(a0 + xa @ a2)
    g = jax.nn.sigmoid(xg) @ g2

    def heads(t):
        return t.reshape(b, s, RWKV_HEADS, RWKV_HEAD_DIM)

    kk = heads(k * k_k).astype(jnp.float32)
    kk = kk * lax.rsqrt(jnp.maximum(jnp.sum(kk * kk, axis=-1, keepdims=True), 1e-24))
    k = k * (1 + (a - 1) * k_a)
    rh, kh, vh, ah = heads(r), heads(k), heads(v), heads(a)
    y = wkv7_scan(rh, heads(decay), kh, vh, -kk, kk * ah)
    ym = jnp.mean(y, axis=-1, keepdims=True)
    yv = jnp.mean(jnp.square(y - ym), axis=-1, keepdims=True)
    y = ((y - ym) * lax.rsqrt(yv + RWKV_GN_EPS)).reshape(b, s, RWKV_WIDTH) * lnx_g + lnx_b
    bonus = jnp.sum(rh * kh * r_k, axis=-1, keepdims=True) * vh
    y = y + bonus.reshape(b, s, RWKV_WIDTH)
    return (y * g).astype(u.dtype), v_first


def hybrid_mixer(u, cos, sin, v_first, w_in, conv_w, rwkv_params, vres, p_a, p_b, p_c, w_o):
    z = jnp.einsum('bsd,dn->bsn', u, w_in)
    z_gate, z_conv, z_rwkv, z_attn = split_cols(z, (3 * D_MODEL, 3 * CONV_WIDTH, RWKV_COLS, 3 * ATTN_WIDTH))
    gate_a, gate_b, gate_c = jnp.split(jax.nn.sigmoid(z_gate), 3, axis=-1)
    conv_b, conv_c, conv_h = jnp.split(z_conv, 3, axis=-1)
    y_a = (conv_b * causal_depthwise_conv(conv_c * conv_h, conv_w)) @ p_a
    y_b, v_first = rwkv7_time_mix(u, z_rwkv, v_first, rwkv_params, vres)
    y_b = y_b @ p_b
    y_c = dilated_attention_branch(z_attn, cos, sin) @ p_c
    return (gate_a * y_a + gate_b * y_b + gate_c * y_c) @ w_o, v_first


def moe_ffn(u, w_router, router_bias, w_gate, w_up, w_down):
    logits = jnp.einsum('bsd,de->bse', u, w_router).astype(jnp.float32)
    probs = jax.nn.softmax(logits, axis=-1)
    sel = probs + router_bias.astype(jnp.float32)
    grouped = sel.reshape(sel.shape[:-1] + (N_EXPERT_GROUPS, EXPERTS_PER_GROUP))
    group_score = jnp.sum(lax.top_k(grouped, TOP_K)[0], axis=-1)
    best_group = jnp.argmax(group_score, axis=-1)
    in_group = (jnp.arange(N_EXPERTS) // EXPERTS_PER_GROUP) == best_group[..., None]
    _, idx = lax.top_k(jnp.where(in_group, sel, -jnp.inf), TOP_K)
    gate_w = jnp.take_along_axis(probs, idx, axis=-1)
    gate_w = gate_w / jnp.sum(gate_w, axis=-1, keepdims=True)
    combine = jnp.sum(jax.nn.one_hot(idx, N_EXPERTS, dtype=jnp.float32) * gate_w[..., None], axis=-2)
    combine = combine.astype(u.dtype)
    out = jnp.zeros_like(u)
    for e in range(N_EXPERTS):
        hid = jax.nn.silu(u @ w_gate[e]) * (u @ w_up[e])
        out = out + combine[..., e:e + 1] * (hid @ w_down[e])
    return out


def setup_inputs(seed: int = 0) -> dict:
    key = jax.random.key(seed)
    keys = jax.random.split(key, 48)
    counter = iter(range(48))

    def nk():
        return keys[next(counter)]

    def nrm(shape, scale):
        return jax.random.normal(nk(), shape, jnp.float32) * scale

    L = DEPTH
    x = nrm((BATCH, SEQ, D_MODEL), 1.0)
    offset = jax.random.randint(nk(), (BATCH, 1), 0, MAX_POS_OFFSET, dtype=jnp.int32)
    positions = offset + jnp.arange(SEQ, dtype=jnp.int32)[None, :]
    return {
        'x': x,
        'positions': positions,
        'ln_in_g': 1.0 + nrm((D_MODEL,), 0.02),
        'ln_in_b': nrm((D_MODEL,), 0.02),
        'w_in': nrm((L, D_MODEL, N_IN), D_MODEL ** -0.5),
        'conv_w': nrm((L, CONV_TAPS, CONV_WIDTH), CONV_TAPS ** -0.5),
        'rwkv_mu': jax.random.uniform(nk(), (L, RWKV_COLS), jnp.float32, 0.0, 1.0),
        'rwkv_w0': jax.random.uniform(nk(), (L, RWKV_WIDTH), jnp.float32, -6.0, -1.0),
        'rwkv_w2': nrm((L, RWKV_DECAY_LORA, RWKV_WIDTH), 0.1 * RWKV_DECAY_LORA ** -0.5),
        'rwkv_a0': nrm((L, RWKV_WIDTH), 0.1),
        'rwkv_a2': nrm((L, RWKV_AAA_LORA, RWKV_WIDTH), RWKV_AAA_LORA ** -0.5),
        'rwkv_g2': nrm((L, RWKV_GATE_LORA, RWKV_WIDTH), RWKV_GATE_LORA ** -0.5),
        'rwkv_v0': 1.0 + nrm((L - 1, RWKV_WIDTH), 0.1),
        'rwkv_v1': nrm((L - 1, D_MODEL, RWKV_MV_LORA), D_MODEL ** -0.5),
        'rwkv_v2': nrm((L - 1, RWKV_MV_LORA, RWKV_WIDTH), RWKV_MV_LORA ** -0.5),
        'rwkv_k_k': 0.85 + nrm((L, RWKV_WIDTH), 0.05),
        'rwkv_k_a': 1.0 + nrm((L, RWKV_WIDTH), 0.05),
        'rwkv_r_k': nrm((L, RWKV_HEADS, RWKV_HEAD_DIM), 0.1),
        'rwkv_lnx_g': 1.0 + nrm((L, RWKV_WIDTH), 0.02),
        'rwkv_lnx_b': nrm((L, RWKV_WIDTH), 0.02),
        'p_a': nrm((L, CONV_WIDTH, D_MODEL), BETA * CONV_WIDTH ** -0.5),
        'p_b': nrm((L, RWKV_WIDTH, D_MODEL), BETA * RWKV_WIDTH ** -0.5),
        'p_c': nrm((L, ATTN_OUT_WIDTH, D_MODEL), BETA * ATTN_OUT_WIDTH ** -0.5),
        'w_o': nrm((L, D_MODEL, D_MODEL), BETA * D_MODEL ** -0.5),
        'ln1_g': 1.0 + nrm((L, D_MODEL), 0.02),
        'ln1_b': nrm((L, D_MODEL), 0.02),
        'w_router': nrm((D_MODEL, N_EXPERTS), D_MODEL ** -0.5),
        'router_bias': nrm((N_EXPERTS,), 0.01),
        'w_gate': nrm((L, N_EXPERTS, D_MODEL, D_EXPERT), D_MODEL ** -0.5),
        'w_up': nrm((L, N_EXPERTS, D_MODEL, D_EXPERT), D_MODEL ** -0.5),
        'w_down': nrm((L, N_EXPERTS, D_EXPERT, D_MODEL), BETA * D_EXPERT ** -0.5),
        'ln2_g': 1.0 + nrm((L, D_MODEL), 0.02),
        'ln2_b': nrm((L, D_MODEL), 0.02),
    }


def reference(x, positions, ln_in_g, ln_in_b, w_in, conv_w, rwkv_mu, rwkv_w0, rwkv_w2, rwkv_a0,
              rwkv_a2, rwkv_g2, rwkv_v0, rwkv_v1, rwkv_v2, rwkv_k_k, rwkv_k_a, rwkv_r_k, rwkv_lnx_g,
              rwkv_lnx_b, p_a, p_b, p_c, w_o, ln1_g, ln1_b, w_router, router_bias, w_gate, w_up,
              w_down, ln2_g, ln2_b):
    h = layer_norm(x, ln_in_g, ln_in_b)
    cos, sin = rotary_tables(positions)
    v_first = None
    for l in range(DEPTH):
        rwkv_params = (rwkv_mu[l], rwkv_w0[l], rwkv_w2[l], rwkv_a0[l], rwkv_a2[l], rwkv_g2[l],
                       rwkv_k_k[l], rwkv_k_a[l], rwkv_r_k[l], rwkv_lnx_g[l], rwkv_lnx_b[l])
        vres = None if l == 0 else (rwkv_v0[l - 1], rwkv_v1[l - 1], rwkv_v2[l - 1])
        mix, v_first = hybrid_mixer(h, cos, sin, v_first, w_in[l], conv_w[l], rwkv_params, vres,
                                    p_a[l], p_b[l], p_c[l], w_o[l])
        h = layer_norm(ALPHA * h + mix, ln1_g[l], ln1_b[l])
        ffn = moe_ffn(h, w_router, router_bias, w_gate[l], w_up[l], w_down[l])
        h = layer_norm(ALPHA * h + ffn, ln2_g[l], ln2_b[l])
    return h
```

```python
import functools

import jax
import jax.numpy as jnp
import numpy as np
from jax import lax
from jax.experimental import pallas as pl
from jax.experimental.pallas import tpu as pltpu

F32 = jnp.float32
BF16 = jnp.bfloat16

D_MODEL = 1024
HEAD_DIM = 64
N_HEADS = D_MODEL // HEAD_DIM
LANES = 128
HEADS_PER_VREG = LANES // HEAD_DIM
N_PAIRS = D_MODEL // LANES
DECAY_LORA = 64
AAA_LORA = 64
GATE_LORA = 160
MV_LORA = 32
LORA_PAD = 384
MV_PAD = 128
GN_EPS = 1e-5 * HEAD_DIM
ATTN_GROUPS = ((128, 1), (512, 4), (2048, 16))
ATTN_HEADS = 8
ATTN_GW = ATTN_HEADS * HEAD_DIM
ATTN_W = len(ATTN_GROUPS) * ATTN_GW
ATTN_BLOCK = 128
ROT_DIM = 16
ROPE_THETA = 500000.0
N_EXPERTS = 16
EXPERTS_PER_GROUP = 4
D_EXPERT = 512
DEPTH = 2
ALPHA = (2 * DEPTH) ** 0.25
LN_EPS = 1e-5
NEG_INF = -1e30
SCAN_CHUNK = 64

OFF_GATE = 0
OFF_CONV = 3072
OFF_RKV = 6144
OFF_Q = 9216
OFF_K = 10752
OFF_V = 12288
OFF_LORA = 13824
N_PACK = 14208

VMEM_LIMIT = 56 * 1024 * 1024


def _cparams(sem):
    return pltpu.CompilerParams(dimension_semantics=sem, vmem_limit_bytes=VMEM_LIMIT)


def _layer_norm(x, g, b):
    mu = jnp.mean(x, axis=-1, keepdims=True)
    xc = x - mu
    var = jnp.mean(xc * xc, axis=-1, keepdims=True)
    return xc * lax.rsqrt(var + LN_EPS) * g + b


def _sigmoid(x):
    return 1.0 / (1.0 + jnp.exp(-x))


def _dot(a, b):
    return jnp.dot(a, b, preferred_element_type=F32)


def _dot_nt(a, b):
    return lax.dot_general(a, b, (((1,), (1,)), ((), ())), preferred_element_type=F32)


def _dot_tn(a, b):
    return lax.dot_general(a, b, (((0,), (0,)), ((), ())), preferred_element_type=F32)


def _ln_kernel(x_ref, g_ref, b_ref, o_ref):
    o_ref[...] = _layer_norm(x_ref[...], g_ref[...], b_ref[...])


def _ln_call(x, g, b, tm=512):
    t, d = x.shape
    return pl.pallas_call(
        _ln_kernel,
        grid=(t // tm,),
        in_specs=[pl.BlockSpec((tm, d), lambda i: (i, 0)),
                  pl.BlockSpec((1, d), lambda i: (0, 0)),
                  pl.BlockSpec((1, d), lambda i: (0, 0))],
        out_specs=pl.BlockSpec((tm, d), lambda i: (i, 0)),
        out_shape=jax.ShapeDtypeStruct((t, d), F32),
        compiler_params=_cparams(("parallel",)),
        name="ln_in",
    )(x, g.reshape(1, d), b.reshape(1, d))


def _inproj_kernel(x_ref, w_ref, o_ref):
    o_ref[...] = _dot(x_ref[...].astype(BF16), w_ref[...]).astype(o_ref.dtype)


def _inproj_call(h, w, tm=512, tn=4736):
    t, d = h.shape
    n = w.shape[1]
    return pl.pallas_call(
        _inproj_kernel,
        grid=(n // tn, t // tm),
        in_specs=[pl.BlockSpec((tm, d), lambda j, i: (i, 0)),
                  pl.BlockSpec((d, tn), lambda j, i: (0, j))],
        out_specs=pl.BlockSpec((tm, tn), lambda j, i: (i, j)),
        out_shape=jax.ShapeDtypeStruct((t, n), BF16),
        compiler_params=_cparams(("parallel", "parallel")),
        name="in_proj",
    )(h, w)


def _shift_rows(x, carry8, n):
    rolled = pltpu.roll(x, n, axis=0)
    cr = pltpu.roll(carry8, n, axis=0)
    row = lax.broadcasted_iota(jnp.int32, (8, x.shape[1]), 0)
    top = jnp.where(row < n, cr, rolled[:8])
    return jnp.concatenate([top, rolled[8:]], axis=0)


def _prep_kernel(has_vres, tiles_per_seq,
                 zc_ref, zr_ref, zl_ref, zq_ref, zk_ref, zv_ref, rc_ref, rm_ref, rp_ref,
                 convw_ref, mu_r_ref, mu_l_ref, w0_ref, w2_ref, a0_ref, a2_ref, g2_ref,
                 kk_ref, ka_ref, rk_ref, seg_ref, *rest):
    if has_vres:
        u_ref, vf_ref, v0_ref, v1_ref, v2_ref = rest[:5]
        rest = rest[5:]
    conv_o, r_o, lw_o, k_o, v_o, a_o, b_o, g_o, bonus_o = rest[:9]
    n_g = len(ATTN_GROUPS)
    qkv_o = rest[9:9 + 3 * n_g]
    rest = rest[9 + 3 * n_g:]
    if not has_vres:
        vf_o = rest[0]
        rest = rest[1:]
    c_conv, c_zr, c_zl, c_rows = rest
    i = pl.program_id(0)

    @pl.when(i % tiles_per_seq == 0)
    def _():
        c_conv[...] = jnp.zeros_like(c_conv)
        c_zr[...] = jnp.zeros_like(c_zr)
        c_zl[...] = jnp.zeros_like(c_zl)

    tm = zc_ref.shape[0]
    d = D_MODEL
    zc = zc_ref[...].astype(F32)
    ch = zc[:, d:2 * d] * zc[:, 2 * d:]
    carry = c_conv[...]
    conv = (convw_ref[0:1, :] * _shift_rows(ch, carry, 2)
            + convw_ref[1:2, :] * _shift_rows(ch, carry, 1)
            + convw_ref[2:3, :] * ch)
    conv_o[...] = (zc[:, :d] * conv).astype(conv_o.dtype)
    c_conv[...] = ch[tm - 8:, :]

    zr = zr_ref[...].astype(F32)
    zl = zl_ref[...].astype(F32)
    zr_m = zr + (_shift_rows(zr, c_zr[...], 1) - zr) * mu_r_ref[...]
    zl_m = zl + (_shift_rows(zl, c_zl[...], 1) - zl) * mu_l_ref[...]
    c_zr[...] = zr[tm - 8:, :]
    c_zl[...] = zl[tm - 8:, :]
    r = zr_m[:, :d]
    k = zr_m[:, d:2 * d]
    v = zr_m[:, 2 * d:]
    x = w0_ref[...] + _dot(jnp.tanh(zl_m).astype(BF16), w2_ref[...])
    nx = -x
    softplus = jnp.maximum(nx, 0.0) + jnp.log(1.0 + jnp.exp(-jnp.abs(nx)))
    lw_o[...] = -jnp.exp(-softplus - 0.5)
    if has_vres:
        lo = _dot(u_ref[...].astype(BF16), v1_ref[...])
        gate_v = _sigmoid(v0_ref[...] + _dot(lo.astype(BF16), v2_ref[...]))
        v = v + (vf_ref[...].astype(F32) - v) * gate_v
    else:
        vf_o[...] = v.astype(vf_o.dtype)
    zl_b = zl_m.astype(BF16)
    a_gate = _sigmoid(a0_ref[...] + _dot(zl_b, a2_ref[...]))
    g_o[...] = _dot(_sigmoid(zl_m).astype(BF16), g2_ref[...]).astype(g_o.dtype)
    seg = seg_ref[...]
    kk = k * kk_ref[...]
    ss = _dot((kk * kk).astype(BF16), seg)
    kk = kk * lax.rsqrt(jnp.maximum(ss, 1e-24))
    k2 = k * (1.0 + (a_gate - 1.0) * ka_ref[...])
    bonus = _dot((r * k2 * rk_ref[...]).astype(BF16), seg) * v
    r_o[...] = r.astype(r_o.dtype)
    k_o[...] = k2.astype(k_o.dtype)
    v_o[...] = v.astype(v_o.dtype)
    a_o[...] = (-kk).astype(a_o.dtype)
    b_o[...] = (kk * a_gate).astype(b_o.dtype)
    bonus_o[...] = bonus.astype(bonus_o.dtype)

    reps = ATTN_W // LANES
    rc = jnp.concatenate([rc_ref[...]] * reps, axis=1)
    rm = jnp.concatenate([rm_ref[...]] * reps, axis=1)
    rp = jnp.concatenate([rp_ref[...]] * reps, axis=1)
    half = ROT_DIM // 2
    for which, (src, scale) in enumerate(((zq_ref, HEAD_DIM ** -0.5), (zk_ref, 1.0), (zv_ref, None))):
        t = src[...].astype(F32)
        if scale is not None:
            t = (t * rc + pltpu.roll(t, ATTN_W - half, axis=1) * rm + pltpu.roll(t, half, axis=1) * rp) * scale
        for gi, (_, dil) in enumerate(ATTN_GROUPS):
            dst = qkv_o[3 * gi + which]
            tg = t[:, gi * ATTN_GW:(gi + 1) * ATTN_GW]
            if dil == 1:
                dst[0] = tg.astype(dst.dtype)
            else:
                for j in range(ATTN_GW // LANES):
                    c_rows[j] = tg[:, j * LANES:(j + 1) * LANES]
                for res in range(dil):
                    for j in range(ATTN_GW // LANES):
                        dst[res, :, j * LANES:(j + 1) * LANES] = (
                            c_rows[j, pl.ds(res, tm // dil, stride=dil), :].astype(dst.dtype))


def _prep_call(z, rot_tabs, lp, u, v_first, batch, seq, tm=256):
    t = z.shape[0]
    d = D_MODEL
    has_vres = v_first is not None
    tps = seq // tm
    row = lambda w, j: pl.BlockSpec((tm, w), lambda i, j=j: (i, j))
    full = lambda a: pl.BlockSpec(a.shape, lambda i: (0,) * a.ndim)
    consts = [lp["conv_w"], lp["mu_r"], lp["mu_l"], lp["w0"], lp["w2"], lp["a0"], lp["a2"], lp["g2"],
              lp["k_k"], lp["k_a"], lp["r_k"], lp["seg"]]
    args = [z, z, z, z, z, z, *rot_tabs, *consts]
    in_specs = [row(3 * d, OFF_CONV // (3 * d)), row(3 * d, OFF_RKV // (3 * d)),
                row(LORA_PAD, OFF_LORA // LORA_PAD), row(ATTN_W, OFF_Q // ATTN_W),
                row(ATTN_W, OFF_K // ATTN_W), row(ATTN_W, OFF_V // ATTN_W),
                row(LANES, 0), row(LANES, 0), row(LANES, 0)] + [full(a) for a in consts]
    if has_vres:
        extra = [u, v_first, lp["v0"], lp["v1"], lp["v2"]]
        args += extra
        in_specs += [row(d, 0), row(d, 0)] + [full(a) for a in extra[2:]]
    out_w = [(d, BF16), (d, BF16), (d, F32), (d, BF16), (d, BF16), (d, BF16), (d, BF16), (d, BF16), (d, BF16)]
    out_specs = [row(w, 0) for w, _ in out_w]
    out_shape = [jax.ShapeDtypeStruct((t, w), dt) for w, dt in out_w]
    for _, dil in ATTN_GROUPS:
        for _ in range(3):
            out_specs.append(pl.BlockSpec((None, dil, tm // dil, ATTN_GW),
                                          lambda i: (i // tps, 0, i % tps, 0)))
            out_shape.append(jax.ShapeDtypeStruct((batch, dil, seq // dil, ATTN_GW), BF16))
    if not has_vres:
        out_specs.append(row(d, 0))
        out_shape.append(jax.ShapeDtypeStruct((t, d), BF16))
    return pl.pallas_call(
        functools.partial(_prep_kernel, has_vres, tps),
        grid=(t // tm,),
        in_specs=in_specs,
        out_specs=out_specs,
        out_shape=out_shape,
        scratch_shapes=[pltpu.VMEM((8, d), F32), pltpu.VMEM((8, 3 * d), F32), pltpu.VMEM((8, LORA_PAD), F32),
                        pltpu.VMEM((ATTN_GW // LANES, tm, LANES), F32)],
        compiler_params=_cparams(("arbitrary",)),
        name="mixer_prep",
    )(*args)


def _scan_kernel(r_ref, lw_ref, k_ref, v_ref, a_ref, b_ref, bonus_ref, g_ref, lng_ref, lnb_ref,
                 o_ref, h_ref):
    c = pl.program_id(1)

    @pl.when(c == 0)
    def _():
        h_ref[...] = jnp.zeros_like(h_ref)

    C = SCAN_CHUNK
    n2 = HEADS_PER_VREG * C
    ri = lax.broadcasted_iota(jnp.int32, (C, C), 0)
    ci = lax.broadcasted_iota(jnp.int32, (C, C), 1)
    tri = (ri >= ci).astype(F32)
    cum_all = jnp.dot(tri, lw_ref[...], preferred_element_type=F32, precision=lax.Precision.HIGHEST)
    lane = lax.broadcasted_iota(jnp.int32, (1, LANES), 1)
    head_masks = [((lane // HEAD_DIM) == hh).astype(F32) for hh in range(HEADS_PER_VREG)]
    r2 = lax.broadcasted_iota(jnp.int32, (n2, n2), 0)
    c2 = lax.broadcasted_iota(jnp.int32, (n2, n2), 1)
    same = (r2 // C) == (c2 // C)
    strict = same & ((r2 % C) > (c2 % C))
    incl = same & ((r2 % C) >= (c2 % C))
    eye = (r2 == c2).astype(F32)
    lr = lax.broadcasted_iota(jnp.int32, (LANES, LANES), 0)
    lc = lax.broadcasted_iota(jnp.int32, (LANES, LANES), 1)
    seg_mean = jnp.where((lr // HEAD_DIM) == (lc // HEAD_DIM), 1.0 / HEAD_DIM, 0.0).astype(F32)

    def stack(x):
        return jnp.concatenate([x * m for m in head_masks], axis=0).astype(BF16)

    for p in range(N_PAIRS):
        sl = slice(p * LANES, (p + 1) * LANES)
        lw = lw_ref[:, sl]
        cum = cum_all[:, sl]
        tot = cum[C - 1:C, :]
        e_fwd = jnp.exp(cum)
        e_bwd = jnp.exp(-cum)
        e_prev = jnp.exp(cum - lw)
        e_rest = jnp.exp(tot - cum)
        r = r_ref[:, sl].astype(F32)
        k = k_ref[:, sl].astype(F32)
        v = v_ref[:, sl].astype(F32)
        a = a_ref[:, sl].astype(F32)
        b = b_ref[:, sl].astype(F32)
        lhs = jnp.concatenate([stack(a * e_prev), stack(r * e_fwd)], axis=0)
        rhs = jnp.concatenate([stack(b * e_bwd), stack(k * e_bwd)], axis=0)
        amat = _dot_nt(lhs, rhs)
        a_ab = jnp.where(strict, amat[:n2, :n2], 0.0)
        a_ak = jnp.where(strict, amat[:n2, n2:], 0.0)
        a_rb = jnp.where(incl, amat[n2:, :n2], 0.0)
        a_rk = jnp.where(incl, amat[n2:, n2:], 0.0)
        tinv = eye + a_ab
        ab = a_ab.astype(BF16)
        pw = _dot(ab, ab)
        levels = int(np.log2(C)) - 1
        for lvl in range(levels):
            pb = pw.astype(BF16)
            if lvl < levels - 1:
                res = _dot(pb, jnp.concatenate([tinv.astype(BF16), pb], axis=1))
                tinv = tinv + res[:, :n2]
                pw = res[:, n2:]
            else:
                tinv = tinv + _dot(pb, tinv.astype(BF16))
        ht = h_ref[p]
        ah = _dot_nt(lhs, ht.astype(BF16))
        v2 = stack(v)
        w = ah[:n2] + _dot(a_ak.astype(BF16), v2)
        u = _dot(tinv.astype(BF16), w.astype(BF16))
        uv = jnp.concatenate([u.astype(BF16), v2], axis=0)
        y2 = ah[n2:] + _dot(jnp.concatenate([a_rb, a_rk], axis=1).astype(BF16), uv)
        y = y2[:C]
        for hh in range(1, HEADS_PER_VREG):
            y = y + y2[hh * C:(hh + 1) * C]
        bk = jnp.concatenate([stack(b * e_rest), stack(k * e_rest)], axis=0)
        h_ref[p] = ht * jnp.exp(tot) + _dot_tn(uv, bk)
        ym = jnp.dot(y, seg_mean, preferred_element_type=F32, precision=lax.Precision.HIGHEST)
        yc = y - ym
        yv = jnp.dot(yc * yc, seg_mean, preferred_element_type=F32, precision=lax.Precision.HIGHEST)
        yn = yc * lax.rsqrt(yv + GN_EPS) * lng_ref[:, sl] + lnb_ref[:, sl]
        out = (yn + bonus_ref[:, sl].astype(F32)) * g_ref[:, sl].astype(F32)
        o_ref[:, sl] = out.astype(o_ref.dtype)


def _scan_call(r, lw, k, v, a, b, bonus, g, lng, lnb, batch, seq):
    t, d = r.shape
    C = SCAN_CHUNK
    nc = seq // C
    blk = pl.BlockSpec((C, d), lambda bi, ci: (bi * nc + ci, 0))
    full = pl.BlockSpec((1, d), lambda bi, ci: (0, 0))
    return pl.pallas_call(
        _scan_kernel,
        grid=(batch, nc),
        in_specs=[blk] * 8 + [full, full],
        out_specs=blk,
        out_shape=jax.ShapeDtypeStruct((t, d), BF16),
        scratch_shapes=[pltpu.VMEM((N_PAIRS, LANES, LANES), F32)],
        compiler_params=_cparams(("parallel", "arbitrary")),
        name="wkv7_scan",
    )(r, lw, k, v, a, b, bonus, g, lng, lnb)


def _attn_kernel(span, q_ref, kp_ref, kc_ref, vp_ref, vc_ref, o_ref, lse_ref):
    n = pl.program_id(2)
    Q = ATTN_BLOCK
    qi = lax.broadcasted_iota(jnp.int32, (Q, 2 * Q), 0)
    kj = lax.broadcasted_iota(jnp.int32, (Q, 2 * Q), 1)
    dist = Q + qi - kj
    valid = (dist >= 0) & (dist <= span) & ((n > 0) | (kj >= Q))
    lane = lax.broadcasted_iota(jnp.int32, (1, LANES), 1)
    masks = [(lane // HEAD_DIM) == hh for hh in range(HEADS_PER_VREG)]
    for p in range(ATTN_GW // LANES):
        sl = slice(p * LANES, (p + 1) * LANES)
        q = q_ref[:, sl]
        kk = jnp.concatenate([kp_ref[:, sl], kc_ref[:, sl]], axis=0)
        vv = jnp.concatenate([vp_ref[:, sl], vc_ref[:, sl]], axis=0)
        out = None
        lse = None
        for m in masks:
            s = _dot_nt(jnp.where(m, q, jnp.zeros_like(q)), kk)
            s = jnp.where(valid, s, NEG_INF)
            mx = jnp.max(s, axis=-1, keepdims=True)
            e = jnp.exp(s - mx)
            den = jnp.sum(e, axis=-1, keepdims=True)
            o = _dot((e / den).astype(BF16), jnp.where(m, vv, jnp.zeros_like(vv)))
            l = mx + jnp.log(den)
            out = o if out is None else out + o
            lse = jnp.broadcast_to(l, (Q, LANES)) if lse is None else jnp.where(m, l, lse)
        o_ref[:, sl] = out.astype(o_ref.dtype)
        lse_ref[:, sl] = lse


def _attn_call(q, k, v, window, dilation):
    batch, _, L, _ = q.shape
    nb = L // ATTN_BLOCK
    Q = ATTN_BLOCK
    cur = pl.BlockSpec((None, None, Q, ATTN_GW), lambda b, r, n: (b, r, n, 0))
    prev = pl.BlockSpec((None, None, Q, ATTN_GW), lambda b, r, n: (b, r, jnp.maximum(n - 1, 0), 0))
    return pl.pallas_call(
        functools.partial(_attn_kernel, window // dilation),
        grid=(batch, dilation, nb),
        in_specs=[cur, prev, cur, prev, cur],
        out_specs=[cur, cur],
        out_shape=[jax.ShapeDtypeStruct(q.shape, BF16), jax.ShapeDtypeStruct(q.shape, F32)],
        compiler_params=_cparams(("parallel", "parallel", "arbitrary")),
        name=f"dilated_attn_d{dilation}",
    )(q, k, k, v, v)


def _out_kernel(conv_ref, yb_ref, o0_ref, o1_ref, o2_ref, l0_ref, l1_ref, l2_ref, zg_ref, h_ref,
                pa_ref, pb_ref, pc_ref, wo_ref, g_ref, b_ref, wr_ref, rb_ref, h1_ref, comb_ref, sc_ref):
    d = D_MODEL
    tm = h_ref.shape[0]

    def token_order(ref, dil):
        if dil == 1:
            return ref[0].astype(F32)
        nl = ATTN_GW // LANES
        for res in range(dil):
            blk = ref[res].astype(F32)
            for j in range(nl):
                sc_ref[j, pl.ds(res, tm // dil, stride=dil), :] = blk[:, j * LANES:(j + 1) * LANES]
        return jnp.concatenate([sc_ref[j] for j in range(nl)], axis=1)

    dils = [dil for _, dil in ATTN_GROUPS]
    l0, l1, l2 = [token_order(r, dl) for r, dl in zip((l0_ref, l1_ref, l2_ref), dils)]
    mx = jnp.maximum(jnp.maximum(l0, l1), l2)
    e0, e1, e2 = jnp.exp(l0 - mx), jnp.exp(l1 - mx), jnp.exp(l2 - mx)
    den = e0 + e1 + e2
    yc = e0 * token_order(o0_ref, dils[0])
    yc = yc + e1 * token_order(o1_ref, dils[1])
    yc = (yc + e2 * token_order(o2_ref, dils[2])) / den
    gates = _sigmoid(zg_ref[...].astype(F32))
    mix = (gates[:, :d] * _dot(conv_ref[...], pa_ref[...])
           + gates[:, d:2 * d] * _dot(yb_ref[...], pb_ref[...])
           + gates[:, 2 * d:] * _dot(yc.astype(BF16), pc_ref[...]))
    out = _dot(mix.astype(BF16), wo_ref[...])
    h1 = _layer_norm(ALPHA * h_ref[...] + out, g_ref[...], b_ref[...])
    h1_ref[...] = h1
    logits = lax.dot_general(wr_ref[...], h1, (((1,), (1,)), ((), ())), preferred_element_type=F32,
                             precision=lax.Precision.HIGHEST)
    mxl = jnp.max(logits, axis=0, keepdims=True)
    ex = jnp.exp(logits - mxl)
    probs = ex / jnp.sum(ex, axis=0, keepdims=True)
    sel = probs + rb_ref[...]
    rows = [sel[e:e + 1, :] for e in range(N_EXPERTS)]
    G = N_EXPERTS // EXPERTS_PER_GROUP
    scores = []
    for gi in range(G):
        a, b, c, dd = rows[4 * gi:4 * gi + 4]
        hi1, lo1 = jnp.maximum(a, b), jnp.minimum(a, b)
        hi2, lo2 = jnp.maximum(c, dd), jnp.minimum(c, dd)
        scores.append(jnp.maximum(hi1, hi2) + jnp.maximum(jnp.minimum(hi1, hi2), jnp.maximum(lo1, lo2)))
    best_score = scores[0]
    best = jnp.zeros_like(scores[0], dtype=jnp.int32)
    for gi in range(1, G):
        better = scores[gi] > best_score
        best = jnp.where(better, gi, best)
        best_score = jnp.where(better, scores[gi], best_score)
    eid = lax.broadcasted_iota(jnp.int32, sel.shape, 0)
    in_group = (eid // EXPERTS_PER_GROUP) == best
    masked = jnp.where(in_group, sel, -jnp.inf)
    m1 = jnp.max(masked, axis=0, keepdims=True)
    i1 = jnp.min(jnp.where(masked == m1, eid, N_EXPERTS), axis=0, keepdims=True)
    masked2 = jnp.where(eid == i1, -jnp.inf, masked)
    m2 = jnp.max(masked2, axis=0, keepdims=True)
    i2 = jnp.min(jnp.where(masked2 == m2, eid, N_EXPERTS), axis=0, keepdims=True)
    p1 = jnp.sum(jnp.where(eid == i1, probs, 0.0), axis=0, keepdims=True)
    p2 = jnp.sum(jnp.where(eid == i2, probs, 0.0), axis=0, keepdims=True)
    tot = p1 + p2
    comb_ref[...] = jnp.where(eid == i1, p1 / tot, 0.0) + jnp.where(eid == i2, p2 / tot, 0.0)


def _out_call(conv, yb, outs, lses, z, h, lp, w_router_t, router_bias, seq, tm=256):
    t, d = h.shape
    tps = seq // tm
    row = lambda w, j=0: pl.BlockSpec((tm, w), lambda i, j=j: (i, j))
    full = lambda a: pl.BlockSpec(a.shape, lambda i: (0,) * a.ndim)
    res = [pl.BlockSpec((None, dil, tm // dil, ATTN_GW), lambda i: (i // tps, 0, i % tps, 0))
           for _, dil in ATTN_GROUPS]
    consts = [lp["p_a"], lp["p_b"], lp["p_c"], lp["w_o"], lp["ln1_g"], lp["ln1_b"], w_router_t, router_bias]
    return pl.pallas_call(
        _out_kernel,
        grid=(t // tm,),
        in_specs=[row(d), row(d)] + res + res + [row(3 * d, OFF_GATE), row(d)]
                 + [full(a) for a in consts],
        out_specs=[row(d), pl.BlockSpec((N_EXPERTS, tm), lambda i: (0, i))],
        out_shape=[jax.ShapeDtypeStruct((t, d), F32), jax.ShapeDtypeStruct((N_EXPERTS, t), F32)],
        scratch_shapes=[pltpu.VMEM((ATTN_GW // LANES, tm, LANES), F32)],
        compiler_params=_cparams(("parallel",)),
        name="mixer_out",
    )(conv, yb, *outs, *lses, z, h, *consts)


def _moe_kernel(x_ref, comb_ref, wg_ref, wu_ref, wd_ref, g_ref, b_ref, o_ref, xb_ref, acc_ref):
    e = pl.program_id(1)

    @pl.when(e == 0)
    def _():
        xb_ref[...] = x_ref[...].astype(BF16)
        acc_ref[...] = jnp.zeros_like(acc_ref)

    xb = xb_ref[...]
    gate = _dot(xb, wg_ref[...])
    up = _dot(xb, wu_ref[...])
    lane = lax.broadcasted_iota(jnp.int32, comb_ref.shape, 1)
    cw = jnp.sum(jnp.where(lane == e, comb_ref[...], 0.0), axis=-1, keepdims=True)
    hid = gate * _sigmoid(gate) * up * cw
    acc_ref[...] += _dot(hid.astype(BF16), wd_ref[...])

    @pl.when(e == N_EXPERTS - 1)
    def _():
        o_ref[...] = _layer_norm(ALPHA * x_ref[...] + acc_ref[...], g_ref[...], b_ref[...])


def _moe_call(h1, comb, lp, tm=1024):
    t, d = h1.shape
    wsp = lambda s: pl.BlockSpec((None,) + s, lambda i, e: (e, 0, 0))
    vec = pl.BlockSpec((1, d), lambda i, e: (0, 0))
    return pl.pallas_call(
        _moe_kernel,
        grid=(t // tm, N_EXPERTS),
        in_specs=[pl.BlockSpec((tm, d), lambda i, e: (i, 0)),
                  pl.BlockSpec((tm, LANES), lambda i, e: (i, 0)),
                  wsp((d, D_EXPERT)), wsp((d, D_EXPERT)), wsp((D_EXPERT, d)), vec, vec],
        out_specs=pl.BlockSpec((tm, d), lambda i, e: (i, 0)),
        out_shape=jax.ShapeDtypeStruct((t, d), F32),
        scratch_shapes=[pltpu.VMEM((tm, d), BF16), pltpu.VMEM((tm, d), F32)],
        compiler_params=_cparams(("parallel", "arbitrary")),
        name="moe",
    )(h1, comb, lp["w_gate"], lp["w_up"], lp["w_down"], lp["ln2_g"], lp["ln2_b"])


def _pad_rows(w, start, total):
    return jnp.zeros((total, w.shape[1]), w.dtype).at[start:start + w.shape[0]].set(w)


def _rotary_tables(positions):
    half = ROT_DIM // 2
    inv_freq = ROPE_THETA ** (-jnp.arange(0, ROT_DIM, 2, dtype=F32) / ROT_DIM)
    ang = positions.astype(F32).reshape(-1, 1) * inv_freq
    cos, sin = jnp.cos(ang), jnp.sin(ang)
    t = ang.shape[0]
    pad = HEAD_DIM - ROT_DIM
    rc = jnp.concatenate([cos, cos, jnp.ones((t, pad), F32)], axis=1)
    rm = jnp.concatenate([-sin, jnp.zeros((t, half + pad), F32)], axis=1)
    rp = jnp.concatenate([jnp.zeros((t, half), F32), sin, jnp.zeros((t, pad), F32)], axis=1)
    return tuple(jnp.tile(x, (1, HEADS_PER_VREG)) for x in (rc, rm, rp))


def _pack_in_proj(w):
    d = D_MODEL
    gate, conv, rwkv, attn = (w[:, :3 * d], w[:, 3 * d:6 * d], w[:, 6 * d:6 * d + 3 * d + 288],
                              w[:, 6 * d + 3 * d + 288:])
    lora = jnp.pad(rwkv[:, 3 * d:], ((0, 0), (0, LORA_PAD - 288)))
    return jnp.concatenate([gate, conv, rwkv[:, :3 * d], attn, lora], axis=1).astype(BF16)


def kernel(x, positions, ln_in_g, ln_in_b, w_in, conv_w, rwkv_mu, rwkv_w0, rwkv_w2, rwkv_a0, rwkv_a2, rwkv_g2, rwkv_v0, rwkv_v1, rwkv_v2, rwkv_k_k, rwkv_k_a, rwkv_r_k, rwkv_lnx_g, rwkv_lnx_b, p_a, p_b, p_c, w_o, ln1_g, ln1_b, w_router, router_bias, w_gate, w_up, w_down, ln2_g, ln2_b):
    batch, seq, d = x.shape
    t = batch * seq
    depth = w_in.shape[0]
    assert d == D_MODEL and seq % (ATTN_BLOCK * ATTN_GROUPS[-1][1]) == 0
    rot_tabs = _rotary_tables(positions)
    hid = lax.broadcasted_iota(jnp.int32, (d, d), 0) // HEAD_DIM
    seg = (hid == hid.T).astype(BF16)
    w_router_t = w_router.T.astype(F32)
    rbias = router_bias.reshape(N_EXPERTS, 1).astype(F32)
    vec = lambda a: a.reshape(1, -1).astype(F32)

    h = _ln_call(x.reshape(t, d), ln_in_g, ln_in_b)
    v_first = None
    for l in range(depth):
        lp = {
            "conv_w": conv_w[l],
            "mu_r": vec(rwkv_mu[l][:3 * d]),
            "mu_l": vec(jnp.pad(rwkv_mu[l][3 * d:], (0, LORA_PAD - 288))),
            "w0": vec(rwkv_w0[l]),
            "w2": _pad_rows(rwkv_w2[l], 0, LORA_PAD).astype(BF16),
            "a0": vec(rwkv_a0[l]),
            "a2": _pad_rows(rwkv_a2[l], DECAY_LORA, LORA_PAD).astype(BF16),
            "g2": _pad_rows(rwkv_g2[l], DECAY_LORA + AAA_LORA, LORA_PAD).astype(BF16),
            "k_k": vec(rwkv_k_k[l]), "k_a": vec(rwkv_k_a[l]), "r_k": vec(rwkv_r_k[l]),
            "seg": seg,
            "p_a": p_a[l].astype(BF16), "p_b": p_b[l].astype(BF16), "p_c": p_c[l].astype(BF16),
            "w_o": w_o[l].astype(BF16), "ln1_g": vec(ln1_g[l]), "ln1_b": vec(ln1_b[l]),
            "w_gate": w_gate[l].astype(BF16), "w_up": w_up[l].astype(BF16), "w_down": w_down[l].astype(BF16),
            "ln2_g": vec(ln2_g[l]), "ln2_b": vec(ln2_b[l]),
        }
        if l > 0:
            lp["v0"] = vec(rwkv_v0[l - 1])
            lp["v1"] = jnp.pad(rwkv_v1[l - 1], ((0, 0), (0, MV_PAD - MV_LORA))).astype(BF16)
            lp["v2"] = _pad_rows(rwkv_v2[l - 1], 0, MV_PAD).astype(BF16)
        z = _inproj_call(h, _pack_in_proj(w_in[l]))
        prep = _prep_call(z, rot_tabs, lp, h if l > 0 else None, v_first, batch, seq)
        conv, r, lw, k, v, a, b, g, bonus = prep[:9]
        qkv = prep[9:9 + 3 * len(ATTN_GROUPS)]
        if l == 0:
            v_first = prep[-1]
        yb = _scan_call(r, lw, k, v, a, b, bonus, g, vec(rwkv_lnx_g[l]), vec(rwkv_lnx_b[l]), batch, seq)
        outs, lses = [], []
        for gi, (window, dilation) in enumerate(ATTN_GROUPS):
            o, s = _attn_call(*qkv[3 * gi:3 * gi + 3], window, dilation)
            outs.append(o)
            lses.append(s)
        h1, comb_t = _out_call(conv, yb, outs, lses, z, h, lp, w_router_t, rbias, seq)
        comb = jnp.pad(comb_t.T, ((0, 0), (0, LANES - N_EXPERTS)))
        h = _moe_call(h1, comb, lp)
    return h.reshape(batch, seq, d)
```

```python
import functools

import jax
import jax.numpy as jnp
import numpy as np
from jax import lax
from jax.experimental import pallas as pl
from jax.experimental.pallas import tpu as pltpu

F32 = jnp.float32
BF16 = jnp.bfloat16

D_MODEL = 1024
HEAD_DIM = 64
N_HEADS = D_MODEL // HEAD_DIM
LANES = 128
HEADS_PER_VREG = LANES // HEAD_DIM
N_PAIRS = D_MODEL // LANES
DECAY_LORA = 64
AAA_LORA = 64
GATE_LORA = 160
MV_LORA = 32
LORA_PAD = 384
MV_PAD = 128
GN_EPS = 1e-5 * HEAD_DIM
ATTN_GROUPS = ((128, 1), (512, 4), (2048, 16))
ATTN_HEADS = 8
ATTN_GW = ATTN_HEADS * HEAD_DIM
ATTN_W = len(ATTN_GROUPS) * ATTN_GW
ATTN_BLOCK = 128
ROT_DIM = 16
ROPE_THETA = 500000.0
N_EXPERTS = 16
EXPERTS_PER_GROUP = 4
D_EXPERT = 512
DEPTH = 2
ALPHA = (2 * DEPTH) ** 0.25
LN_EPS = 1e-5
NEG_INF = -1e30
SCAN_CHUNK = 64

OFF_GATE = 0
OFF_CONV = 3072
OFF_RKV = 6144
OFF_Q = 9216
OFF_K = 10752
OFF_V = 12288
OFF_LORA = 13824
N_PACK = 14208

VMEM_LIMIT = 56 * 1024 * 1024


def _cparams(sem):
    return pltpu.CompilerParams(dimension_semantics=sem, vmem_limit_bytes=VMEM_LIMIT)


def _layer_norm(x, g, b):
    mu = jnp.mean(x, axis=-1, keepdims=True)
    xc = x - mu
    var = jnp.mean(xc * xc, axis=-1, keepdims=True)
    return xc * lax.rsqrt(var + LN_EPS) * g + b


def _sigmoid(x):
    return 1.0 / (1.0 + jnp.exp(-x))


def _dot(a, b):
    return jnp.dot(a, b, preferred_element_type=F32)


def _dot_nt(a, b):
    return lax.dot_general(a, b, (((1,), (1,)), ((), ())), preferred_element_type=F32)


def _dot_tn(a, b):
    return lax.dot_general(a, b, (((0,), (0,)), ((), ())), preferred_element_type=F32)


def _ln_kernel(x_ref, g_ref, b_ref, o_ref):
    o_ref[...] = _layer_norm(x_ref[...], g_ref[...], b_ref[...])


def _ln_call(x, g, b, tm=512):
    t, d = x.shape
    return pl.pallas_call(
        _ln_kernel,
        grid=(t // tm,),
        in_specs=[pl.BlockSpec((tm, d), lambda i: (i, 0)),
                  pl.BlockSpec((1, d), lambda i: (0, 0)),
                  pl.BlockSpec((1, d), lambda i: (0, 0))],
        out_specs=pl.BlockSpec((tm, d), lambda i: (i, 0)),
        out_shape=jax.ShapeDtypeStruct((t, d), F32),
        compiler_params=_cparams(("parallel",)),
        name="ln_in",
    )(x, g.reshape(1, d), b.reshape(1, d))


def _inproj_kernel(x_ref, w_ref, o_ref):
    o_ref[...] = _dot(x_ref[...].astype(BF16), w_ref[...]).astype(o_ref.dtype)


def _inproj_call(h, w, tm=512, tn=4736):
    t, d = h.shape
    n = w.shape[1]
    return pl.pallas_call(
        _inproj_kernel,
        grid=(n // tn, t // tm),
        in_specs=[pl.BlockSpec((tm, d), lambda j, i: (i, 0)),
                  pl.BlockSpec((d, tn), lambda j, i: (0, j))],
        out_specs=pl.BlockSpec((tm, tn), lambda j, i: (i, j)),
        out_shape=jax.ShapeDtypeStruct((t, n), BF16),
        compiler_params=_cparams(("parallel", "parallel")),
        name="in_proj",
    )(h, w)


def _shift_rows(x, carry8, n):
    rolled = pltpu.roll(x, n, axis=0)
    cr = pltpu.roll(carry8, n, axis=0)
    row = lax.broadcasted_iota(jnp.int32, (8, x.shape[1]), 0)
    top = jnp.where(row < n, cr, rolled[:8])
    return jnp.concatenate([top, rolled[8:]], axis=0)


def _prep_kernel(has_vres, tiles_per_seq,
                 zc_ref, zr_ref, zl_ref, zq_ref, zk_ref, zv_ref, rc_ref, rm_ref, rp_ref,
                 convw_ref, mu_r_ref, mu_l_ref, w0_ref, w2_ref, a0_ref, a2_ref, g2_ref,
                 kk_ref, ka_ref, rk_ref, seg_ref, *rest):
    if has_vres:
        u_ref, vf_ref, v0_ref, v1_ref, v2_ref = rest[:5]
        rest = rest[5:]
    conv_o, r_o, lw_o, k_o, v_o, a_o, b_o, g_o, bonus_o = rest[:9]
    n_g = len(ATTN_GROUPS)
    qkv_o = rest[9:9 + 3 * n_g]
    rest = rest[9 + 3 * n_g:]
    if not has_vres:
        vf_o = rest[0]
        rest = rest[1:]
    c_conv, c_zr, c_zl, c_rows = rest
    i = pl.program_id(0)

    @pl.when(i % tiles_per_seq == 0)
    def _():
        c_conv[...] = jnp.zeros_like(c_conv)
        c_zr[...] = jnp.zeros_like(c_zr)
        c_zl[...] = jnp.zeros_like(c_zl)

    tm = zc_ref.shape[0]
    d = D_MODEL
    zc = zc_ref[...].astype(F32)
    ch = zc[:, d:2 * d] * zc[:, 2 * d:]
    carry = c_conv[...]
    conv = (convw_ref[0:1, :] * _shift_rows(ch, carry, 2)
            + convw_ref[1:2, :] * _shift_rows(ch, carry, 1)
            + convw_ref[2:3, :] * ch)
    conv_o[...] = (zc[:, :d] * conv).astype(conv_o.dtype)
    c_conv[...] = ch[tm - 8:, :]

    zr = zr_ref[...].astype(F32)
    zl = zl_ref[...].astype(F32)
    zr_m = zr + (_shift_rows(zr, c_zr[...], 1) - zr) * mu_r_ref[...]
    zl_m = zl + (_shift_rows(zl, c_zl[...], 1) - zl) * mu_l_ref[...]
    c_zr[...] = zr[tm - 8:, :]
    c_zl[...] = zl[tm - 8:, :]
    r = zr_m[:, :d]
    k = zr_m[:, d:2 * d]
    v = zr_m[:, 2 * d:]
    x = w0_ref[...] + _dot(jnp.tanh(zl_m).astype(BF16), w2_ref[...])
    nx = -x
    softplus = jnp.maximum(nx, 0.0) + jnp.log(1.0 + jnp.exp(-jnp.abs(nx)))
    lw_o[...] = -jnp.exp(-softplus - 0.5)
    if has_vres:
        lo = _dot(u_ref[...].astype(BF16), v1_ref[...])
        gate_v = _sigmoid(v0_ref[...] + _dot(lo.astype(BF16), v2_ref[...]))
        v = v + (vf_ref[...].astype(F32) - v) * gate_v
    else:
        vf_o[...] = v.astype(vf_o.dtype)
    zl_b = zl_m.astype(BF16)
    a_gate = _sigmoid(a0_ref[...] + _dot(zl_b, a2_ref[...]))
    g_o[...] = _dot(_sigmoid(zl_m).astype(BF16), g2_ref[...]).astype(g_o.dtype)
    seg = seg_ref[...]
    kk = k * kk_ref[...]
    ss = _dot((kk * kk).astype(BF16), seg)
    kk = kk * lax.rsqrt(jnp.maximum(ss, 1e-24))
    k2 = k * (1.0 + (a_gate - 1.0) * ka_ref[...])
    bonus = _dot((r * k2 * rk_ref[...]).astype(BF16), seg) * v
    r_o[...] = r.astype(r_o.dtype)
    k_o[...] = k2.astype(k_o.dtype)
    v_o[...] = v.astype(v_o.dtype)
    a_o[...] = (-kk).astype(a_o.dtype)
    b_o[...] = (kk * a_gate).astype(b_o.dtype)
    bonus_o[...] = bonus.astype(bonus_o.dtype)

    reps = ATTN_W // LANES
    rc = jnp.concatenate([rc_ref[...]] * reps, axis=1)
    rm = jnp.concatenate([rm_ref[...]] * reps, axis=1)
    rp = jnp.concatenate([rp_ref[...]] * reps, axis=1)
    half = ROT_DIM // 2
    for which, (src, scale) in enumerate(((zq_ref, HEAD_DIM ** -0.5), (zk_ref, 1.0), (zv_ref, None))):
        t = src[...].astype(F32)
        if scale is not None:
            t = (t * rc + pltpu.roll(t, ATTN_W - half, axis=1) * rm + pltpu.roll(t, half, axis=1) * rp) * scale
        for gi, (_, dil) in enumerate(ATTN_GROUPS):
            dst = qkv_o[3 * gi + which]
            tg = t[:, gi * ATTN_GW:(gi + 1) * ATTN_GW]
            if dil == 1:
                dst[0] = tg.astype(dst.dtype)
            else:
                for j in range(ATTN_GW // LANES):
                    c_rows[j] = tg[:, j * LANES:(j + 1) * LANES]
                for res in range(dil):
                    for j in range(ATTN_GW // LANES):
                        dst[res, :, j * LANES:(j + 1) * LANES] = (
                            c_rows[j, pl.ds(res, tm // dil, stride=dil), :].astype(dst.dtype))


def _prep_call(z, rot_tabs, lp, u, v_first, batch, seq, tm=256):
    t = z.shape[0]
    d = D_MODEL
    has_vres = v_first is not None
    tps = seq // tm
    row = lambda w, j: pl.BlockSpec((tm, w), lambda i, j=j: (i, j))
    full = lambda a: pl.BlockSpec(a.shape, lambda i: (0,) * a.ndim)
    consts = [lp["conv_w"], lp["mu_r"], lp["mu_l"], lp["w0"], lp["w2"], lp["a0"], lp["a2"], lp["g2"],
              lp["k_k"], lp["k_a"], lp["r_k"], lp["seg"]]
    args = [z, z, z, z, z, z, *rot_tabs, *consts]
    in_specs = [row(3 * d, OFF_CONV // (3 * d)), row(3 * d, OFF_RKV // (3 * d)),
                row(LORA_PAD, OFF_LORA // LORA_PAD), row(ATTN_W, OFF_Q // ATTN_W),
                row(ATTN_W, OFF_K // ATTN_W), row(ATTN_W, OFF_V // ATTN_W),
                row(LANES, 0), row(LANES, 0), row(LANES, 0)] + [full(a) for a in consts]
    if has_vres:
        extra = [u, v_first, lp["v0"], lp["v1"], lp["v2"]]
        args += extra
        in_specs += [row(d, 0), row(d, 0)] + [full(a) for a in extra[2:]]
    out_w = [(d, BF16), (d, BF16), (d, F32), (d, BF16), (d, BF16), (d, BF16), (d, BF16), (d, BF16), (d, BF16)]
    out_specs = [row(w, 0) for w, _ in out_w]
    out_shape = [jax.ShapeDtypeStruct((t, w), dt) for w, dt in out_w]
    for _, dil in ATTN_GROUPS:
        for _ in range(3):
            out_specs.append(pl.BlockSpec((None, dil, tm // dil, ATTN_GW),
                                          lambda i: (i // tps, 0, i % tps, 0)))
            out_shape.append(jax.ShapeDtypeStruct((batch, dil, seq // dil, ATTN_GW), BF16))
    if not has_vres:
        out_specs.append(row(d, 0))
        out_shape.append(jax.ShapeDtypeStruct((t, d), BF16))
    return pl.pallas_call(
        functools.partial(_prep_kernel, has_vres, tps),
        grid=(t // tm,),
        in_specs=in_specs,
        out_specs=out_specs,
        out_shape=out_shape,
        scratch_shapes=[pltpu.VMEM((8, d), F32), pltpu.VMEM((8, 3 * d), F32), pltpu.VMEM((8, LORA_PAD), F32),
                        pltpu.VMEM((ATTN_GW // LANES, tm, LANES), F32)],
        compiler_params=_cparams(("arbitrary",)),
        name="mixer_prep",
    )(*args)


def _scan_kernel(r_ref, lw_ref, k_ref, v_ref, a_ref, b_ref, bonus_ref, g_ref, lng_ref, lnb_ref,
                 o_ref, h_ref):
    c = pl.program_id(1)

    @pl.when(c == 0)
    def _():
        h_ref[...] = jnp.zeros_like(h_ref)

    C = SCAN_CHUNK
    n2 = HEADS_PER_VREG * C
    ri = lax.broadcasted_iota(jnp.int32, (C, C), 0)
    ci = lax.broadcasted_iota(jnp.int32, (C, C), 1)
    tri = (ri >= ci).astype(F32)
    cum_all = jnp.dot(tri, lw_ref[...], preferred_element_type=F32, precision=lax.Precision.HIGHEST)
    lane = lax.broadcasted_iota(jnp.int32, (1, LANES), 1)
    head_masks = [((lane // HEAD_DIM) == hh).astype(F32) for hh in range(HEADS_PER_VREG)]
    r2 = lax.broadcasted_iota(jnp.int32, (n2, n2), 0)
    c2 = lax.broadcasted_iota(jnp.int32, (n2, n2), 1)
    same = (r2 // C) == (c2 // C)
    strict = same & ((r2 % C) > (c2 % C))
    incl = same & ((r2 % C) >= (c2 % C))
    eye = (r2 == c2).astype(F32)
    lr = lax.broadcasted_iota(jnp.int32, (LANES, LANES), 0)
    lc = lax.broadcasted_iota(jnp.int32, (LANES, LANES), 1)
    seg_mean = jnp.where((lr // HEAD_DIM) == (lc // HEAD_DIM), 1.0 / HEAD_DIM, 0.0).astype(F32)

    def stack(x):
        return jnp.concatenate([x * m for m in head_masks], axis=0).astype(BF16)

    pairs = range(N_PAIRS)
    sls = [slice(p * LANES, (p + 1) * LANES) for p in pairs]
    lhs, rhs, bk, v2, dec = [], [], [], [], []
    for sl in sls:
        lw = lw_ref[:, sl]
        cum = cum_all[:, sl]
        tot = cum[C - 1:C, :]
        e_bwd = jnp.exp(-cum)
        e_rest = jnp.exp(tot - cum)
        r = r_ref[:, sl].astype(F32)
        k = k_ref[:, sl].astype(F32)
        a = a_ref[:, sl].astype(F32)
        b = b_ref[:, sl].astype(F32)
        lhs.append(jnp.concatenate([stack(a * jnp.exp(cum - lw)), stack(r * jnp.exp(cum))], axis=0))
        rhs.append(jnp.concatenate([stack(b * e_bwd), stack(k * e_bwd)], axis=0))
        bk.append(jnp.concatenate([stack(b * e_rest), stack(k * e_rest)], axis=0))
        v2.append(stack(v_ref[:, sl].astype(F32)))
        dec.append(jnp.exp(tot))
    amat = [_dot_nt(lhs[p], rhs[p]) for p in pairs]
    a_ab = [jnp.where(strict, m[:n2, :n2], 0.0) for m in amat]
    a_ak = [jnp.where(strict, m[:n2, n2:], 0.0).astype(BF16) for m in amat]
    a_r = [jnp.concatenate([jnp.where(incl, m[n2:, :n2], 0.0), jnp.where(incl, m[n2:, n2:], 0.0)],
                           axis=1).astype(BF16) for m in amat]
    tinv = [eye + m for m in a_ab]
    ab = [m.astype(BF16) for m in a_ab]
    pw = [_dot(m, m) for m in ab]
    levels = int(np.log2(C)) - 1
    for lvl in range(levels):
        pb = [m.astype(BF16) for m in pw]
        if lvl < levels - 1:
            res = [_dot(pb[p], jnp.concatenate([tinv[p].astype(BF16), pb[p]], axis=1)) for p in pairs]
            tinv = [tinv[p] + res[p][:, :n2] for p in pairs]
            pw = [m[:, n2:] for m in res]
        else:
            tinv = [tinv[p] + _dot(pb[p], tinv[p].astype(BF16)) for p in pairs]
    ht = [h_ref[p] for p in pairs]
    ah = [_dot_nt(lhs[p], ht[p].astype(BF16)) for p in pairs]
    w = [ah[p][:n2] + _dot(a_ak[p], v2[p]) for p in pairs]
    u = [_dot(tinv[p].astype(BF16), w[p].astype(BF16)) for p in pairs]
    uv = [jnp.concatenate([u[p].astype(BF16), v2[p]], axis=0) for p in pairs]
    y2 = [ah[p][n2:] + _dot(a_r[p], uv[p]) for p in pairs]
    for p in pairs:
        h_ref[p] = ht[p] * dec[p] + _dot_tn(uv[p], bk[p])
    ys = []
    for m in y2:
        y = m[:C]
        for hh in range(1, HEADS_PER_VREG):
            y = y + m[hh * C:(hh + 1) * C]
        ys.append(y)
    y = jnp.concatenate(ys, axis=0)
    ym = jnp.dot(y, seg_mean, preferred_element_type=F32, precision=lax.Precision.HIGHEST)
    yc = y - ym
    yv = jnp.dot(yc * yc, seg_mean, preferred_element_type=F32, precision=lax.Precision.HIGHEST)
    yn = yc * lax.rsqrt(yv + GN_EPS)
    for p, sl in enumerate(sls):
        out = yn[p * C:(p + 1) * C] * lng_ref[:, sl] + lnb_ref[:, sl]
        out = (out + bonus_ref[:, sl].astype(F32)) * g_ref[:, sl].astype(F32)
        o_ref[:, sl] = out.astype(o_ref.dtype)


def _scan_call(r, lw, k, v, a, b, bonus, g, lng, lnb, batch, seq):
    t, d = r.shape
    C = SCAN_CHUNK
    nc = seq // C
    blk = pl.BlockSpec((C, d), lambda bi, ci: (bi * nc + ci, 0))
    full = pl.BlockSpec((1, d), lambda bi, ci: (0, 0))
    return pl.pallas_call(
        _scan_kernel,
        grid=(batch, nc),
        in_specs=[blk] * 8 + [full, full],
        out_specs=blk,
        out_shape=jax.ShapeDtypeStruct((t, d), BF16),
        scratch_shapes=[pltpu.VMEM((N_PAIRS, LANES, LANES), F32)],
        compiler_params=_cparams(("parallel", "arbitrary")),
        name="wkv7_scan",
    )(r, lw, k, v, a, b, bonus, g, lng, lnb)


def _attn_kernel(span, q_ref, kp_ref, kc_ref, vp_ref, vc_ref, o_ref, lse_ref):
    n = pl.program_id(2)
    Q = ATTN_BLOCK
    qi = lax.broadcasted_iota(jnp.int32, (Q, 2 * Q), 0)
    kj = lax.broadcasted_iota(jnp.int32, (Q, 2 * Q), 1)
    dist = Q + qi - kj
    valid = (dist >= 0) & (dist <= span) & ((n > 0) | (kj >= Q))
    lane = lax.broadcasted_iota(jnp.int32, (1, LANES), 1)
    masks = [(lane // HEAD_DIM) == hh for hh in range(HEADS_PER_VREG)]
    sls = [slice(p * LANES, (p + 1) * LANES) for p in range(ATTN_GW // LANES)]
    nh = len(masks)
    scores = []
    for sl in sls:
        q = q_ref[:, sl]
        kk = jnp.concatenate([kp_ref[:, sl], kc_ref[:, sl]], axis=0)
        for m in masks:
            scores.append(_dot_nt(jnp.where(m, q, jnp.zeros_like(q)), kk))
    probs, lses = [], []
    for s in scores:
        s = jnp.where(valid, s, NEG_INF)
        mx = jnp.max(s, axis=-1, keepdims=True)
        e = jnp.exp(s - mx)
        den = jnp.sum(e, axis=-1, keepdims=True)
        probs.append((e / den).astype(BF16))
        lses.append(mx + jnp.log(den))
    for p, sl in enumerate(sls):
        vv = jnp.concatenate([vp_ref[:, sl], vc_ref[:, sl]], axis=0)
        out = None
        lse = None
        for hh, m in enumerate(masks):
            o = _dot(probs[p * nh + hh], jnp.where(m, vv, jnp.zeros_like(vv)))
            l = lses[p * nh + hh]
            out = o if out is None else out + o
            lse = jnp.broadcast_to(l, (Q, LANES)) if lse is None else jnp.where(m, l, lse)
        o_ref[:, sl] = out.astype(o_ref.dtype)
        lse_ref[:, sl] = lse


def _attn_call(q, k, v, window, dilation):
    batch, _, L, _ = q.shape
    nb = L // ATTN_BLOCK
    Q = ATTN_BLOCK
    cur = pl.BlockSpec((None, None, Q, ATTN_GW), lambda b, r, n: (b, r, n, 0))
    prev = pl.BlockSpec((None, None, Q, ATTN_GW), lambda b, r, n: (b, r, jnp.maximum(n - 1, 0), 0))
    return pl.pallas_call(
        functools.partial(_attn_kernel, window // dilation),
        grid=(batch, dilation, nb),
        in_specs=[cur, prev, cur, prev, cur],
        out_specs=[cur, cur],
        out_shape=[jax.ShapeDtypeStruct(q.shape, BF16), jax.ShapeDtypeStruct(q.shape, F32)],
        compiler_params=_cparams(("parallel", "parallel", "arbitrary")),
        name=f"dilated_attn_d{dilation}",
    )(q, k, k, v, v)


def _out_kernel(conv_ref, yb_ref, o0_ref, o1_ref, o2_ref, l0_ref, l1_ref, l2_ref, zg_ref, h_ref,
                pa_ref, pb_ref, pc_ref, wo_ref, g_ref, b_ref, wr_ref, rb_ref, h1_ref, comb_ref, sc_ref):
    d = D_MODEL
    tm = h_ref.shape[0]

    def token_order(ref, dil):
        if dil == 1:
            return ref[0].astype(F32)
        nl = ATTN_GW // LANES
        for res in range(dil):
            blk = ref[res].astype(F32)
            for j in range(nl):
                sc_ref[j, pl.ds(res, tm // dil, stride=dil), :] = blk[:, j * LANES:(j + 1) * LANES]
        return jnp.concatenate([sc_ref[j] for j in range(nl)], axis=1)

    dils = [dil for _, dil in ATTN_GROUPS]
    l0, l1, l2 = [token_order(r, dl) for r, dl in zip((l0_ref, l1_ref, l2_ref), dils)]
    mx = jnp.maximum(jnp.maximum(l0, l1), l2)
    e0, e1, e2 = jnp.exp(l0 - mx), jnp.exp(l1 - mx), jnp.exp(l2 - mx)
    den = e0 + e1 + e2
    yc = e0 * token_order(o0_ref, dils[0])
    yc = yc + e1 * token_order(o1_ref, dils[1])
    yc = (yc + e2 * token_order(o2_ref, dils[2])) / den
    gates = _sigmoid(zg_ref[...].astype(F32))
    mix = (gates[:, :d] * _dot(conv_ref[...], pa_ref[...])
           + gates[:, d:2 * d] * _dot(yb_ref[...], pb_ref[...])
           + gates[:, 2 * d:] * _dot(yc.astype(BF16), pc_ref[...]))
    out = _dot(mix.astype(BF16), wo_ref[...])
    h1 = _layer_norm(ALPHA * h_ref[...] + out, g_ref[...], b_ref[...])
    h1_ref[...] = h1
    logits = lax.dot_general(wr_ref[...], h1, (((1,), (1,)), ((), ())), preferred_element_type=F32,
                             precision=lax.Precision.HIGHEST)
    mxl = jnp.max(logits, axis=0, keepdims=True)
    ex = jnp.exp(logits - mxl)
    probs = ex / jnp.sum(ex, axis=0, keepdims=True)
    sel = probs + rb_ref[...]
    rows = [sel[e:e + 1, :] for e in range(N_EXPERTS)]
    G = N_EXPERTS // EXPERTS_PER_GROUP
    scores = []
    for gi in range(G):
        a, b, c, dd = rows[4 * gi:4 * gi + 4]
        hi1, lo1 = jnp.maximum(a, b), jnp.minimum(a, b)
        hi2, lo2 = jnp.maximum(c, dd), jnp.minimum(c, dd)
        scores.append(jnp.maximum(hi1, hi2) + jnp.maximum(jnp.minimum(hi1, hi2), jnp.maximum(lo1, lo2)))
    best_score = scores[0]
    best = jnp.zeros_like(scores[0], dtype=jnp.int32)
    for gi in range(1, G):
        better = scores[gi] > best_score
        best = jnp.where(better, gi, best)
        best_score = jnp.where(better, scores[gi], best_score)
    eid = lax.broadcasted_iota(jnp.int32, sel.shape, 0)
    in_group = (eid // EXPERTS_PER_GROUP) == best
    masked = jnp.where(in_group, sel, -jnp.inf)
    m1 = jnp.max(masked, axis=0, keepdims=True)
    i1 = jnp.min(jnp.where(masked == m1, eid, N_EXPERTS), axis=0, keepdims=True)
    masked2 = jnp.where(eid == i1, -jnp.inf, masked)
    m2 = jnp.max(masked2, axis=0, keepdims=True)
    i2 = jnp.min(jnp.where(masked2 == m2, eid, N_EXPERTS), axis=0, keepdims=True)
    p1 = jnp.sum(jnp.where(eid == i1, probs, 0.0), axis=0, keepdims=True)
    p2 = jnp.sum(jnp.where(eid == i2, probs, 0.0), axis=0, keepdims=True)
    tot = p1 + p2
    comb_ref[...] = jnp.where(eid == i1, p1 / tot, 0.0) + jnp.where(eid == i2, p2 / tot, 0.0)


def _out_call(conv, yb, outs, lses, z, h, lp, w_router_t, router_bias, seq, tm=256):
    t, d = h.shape
    tps = seq // tm
    row = lambda w, j=0: pl.BlockSpec((tm, w), lambda i, j=j: (i, j))
    full = lambda a: pl.BlockSpec(a.shape, lambda i: (0,) * a.ndim)
    res = [pl.BlockSpec((None, dil, tm // dil, ATTN_GW), lambda i: (i // tps, 0, i % tps, 0))
           for _, dil in ATTN_GROUPS]
    consts = [lp["p_a"], lp["p_b"], lp["p_c"], lp["w_o"], lp["ln1_g"], lp["ln1_b"], w_router_t, router_bias]
    return pl.pallas_call(
        _out_kernel,
        grid=(t // tm,),
        in_specs=[row(d), row(d)] + res + res + [row(3 * d, OFF_GATE), row(d)]
                 + [full(a) for a in consts],
        out_specs=[row(d), pl.BlockSpec((N_EXPERTS, tm), lambda i: (0, i))],
        out_shape=[jax.ShapeDtypeStruct((t, d), F32), jax.ShapeDtypeStruct((N_EXPERTS, t), F32)],
        scratch_shapes=[pltpu.VMEM((ATTN_GW // LANES, tm, LANES), F32)],
        compiler_params=_cparams(("parallel",)),
        name="mixer_out",
    )(conv, yb, *outs, *lses, z, h, *consts)


def _moe_kernel(x_ref, comb_ref, wg_ref, wu_ref, wd_ref, g_ref, b_ref, o_ref, xb_ref, acc_ref):
    e = pl.program_id(1)

    @pl.when(e == 0)
    def _():
        xb_ref[...] = x_ref[...].astype(BF16)
        acc_ref[...] = jnp.zeros_like(acc_ref)

    xb = xb_ref[...]
    gate = _dot(xb, wg_ref[...])
    up = _dot(xb, wu_ref[...])
    lane = lax.broadcasted_iota(jnp.int32, comb_ref.shape, 1)
    cw = jnp.sum(jnp.where(lane == e, comb_ref[...], 0.0), axis=-1, keepdims=True)
    hid = gate * _sigmoid(gate) * up * cw
    acc_ref[...] += _dot(hid.astype(BF16), wd_ref[...])

    @pl.when(e == N_EXPERTS - 1)
    def _():
        o_ref[...] = _layer_norm(ALPHA * x_ref[...] + acc_ref[...], g_ref[...], b_ref[...])


def _moe_call(h1, comb, lp, tm=1024):
    t, d = h1.shape
    wsp = lambda s: pl.BlockSpec((None,) + s, lambda i, e: (e, 0, 0))
    vec = pl.BlockSpec((1, d), lambda i, e: (0, 0))
    return pl.pallas_call(
        _moe_kernel,
        grid=(t // tm, N_EXPERTS),
        in_specs=[pl.BlockSpec((tm, d), lambda i, e: (i, 0)),
                  pl.BlockSpec((tm, LANES), lambda i, e: (i, 0)),
                  wsp((d, D_EXPERT)), wsp((d, D_EXPERT)), wsp((D_EXPERT, d)), vec, vec],
        out_specs=pl.BlockSpec((tm, d), lambda i, e: (i, 0)),
        out_shape=jax.ShapeDtypeStruct((t, d), F32),
        scratch_shapes=[pltpu.VMEM((tm, d), BF16), pltpu.VMEM((tm, d), F32)],
        compiler_params=_cparams(("parallel", "arbitrary")),
        name="moe",
    )(h1, comb, lp["w_gate"], lp["w_up"], lp["w_down"], lp["ln2_g"], lp["ln2_b"])


def _pad_rows(w, start, total):
    return jnp.zeros((total, w.shape[1]), w.dtype).at[start:start + w.shape[0]].set(w)


def _rotary_tables(positions):
    half = ROT_DIM // 2
    inv_freq = ROPE_THETA ** (-jnp.arange(0, ROT_DIM, 2, dtype=F32) / ROT_DIM)
    ang = positions.astype(F32).reshape(-1, 1) * inv_freq
    cos, sin = jnp.cos(ang), jnp.sin(ang)
    t = ang.shape[0]
    pad = HEAD_DIM - ROT_DIM
    rc = jnp.concatenate([cos, cos, jnp.ones((t, pad), F32)], axis=1)
    rm = jnp.concatenate([-sin, jnp.zeros((t, half + pad), F32)], axis=1)
    rp = jnp.concatenate([jnp.zeros((t, half), F32), sin, jnp.zeros((t, pad), F32)], axis=1)
    return tuple(jnp.tile(x, (1, HEADS_PER_VREG)) for x in (rc, rm, rp))


def _pack_in_proj(w):
    d = D_MODEL
    gate, conv, rwkv, attn = (w[:, :3 * d], w[:, 3 * d:6 * d], w[:, 6 * d:6 * d + 3 * d + 288],
                              w[:, 6 * d + 3 * d + 288:])
    lora = jnp.pad(rwkv[:, 3 * d:], ((0, 0), (0, LORA_PAD - 288)))
    return jnp.concatenate([gate, conv, rwkv[:, :3 * d], attn, lora], axis=1).astype(BF16)


def kernel(x, positions, ln_in_g, ln_in_b, w_in, conv_w, rwkv_mu, rwkv_w0, rwkv_w2, rwkv_a0, rwkv_a2, rwkv_g2, rwkv_v0, rwkv_v1, rwkv_v2, rwkv_k_k, rwkv_k_a, rwkv_r_k, rwkv_lnx_g, rwkv_lnx_b, p_a, p_b, p_c, w_o, ln1_g, ln1_b, w_router, router_bias, w_gate, w_up, w_down, ln2_g, ln2_b):
    batch, seq, d = x.shape
    t = batch * seq
    depth = w_in.shape[0]
    assert d == D_MODEL and seq % (ATTN_BLOCK * ATTN_GROUPS[-1][1]) == 0
    rot_tabs = _rotary_tables(positions)
    hid = lax.broadcasted_iota(jnp.int32, (d, d), 0) // HEAD_DIM
    seg = (hid == hid.T).astype(BF16)
    w_router_t = w_router.T.astype(F32)
    rbias = router_bias.reshape(N_EXPERTS, 1).astype(F32)
    vec = lambda a: a.reshape(1, -1).astype(F32)

    h = _ln_call(x.reshape(t, d), ln_in_g, ln_in_b)
    v_first = None
    for l in range(depth):
        lp = {
            "conv_w": conv_w[l],
            "mu_r": vec(rwkv_mu[l][:3 * d]),
            "mu_l": vec(jnp.pad(rwkv_mu[l][3 * d:], (0, LORA_PAD - 288))),
            "w0": vec(rwkv_w0[l]),
            "w2": _pad_rows(rwkv_w2[l], 0, LORA_PAD).astype(BF16),
            "a0": vec(rwkv_a0[l]),
            "a2": _pad_rows(rwkv_a2[l], DECAY_LORA, LORA_PAD).astype(BF16),
            "g2": _pad_rows(rwkv_g2[l], DECAY_LORA + AAA_LORA, LORA_PAD).astype(BF16),
            "k_k": vec(rwkv_k_k[l]), "k_a": vec(rwkv_k_a[l]), "r_k": vec(rwkv_r_k[l]),
            "seg": seg,
            "p_a": p_a[l].astype(BF16), "p_b": p_b[l].astype(BF16), "p_c": p_c[l].astype(BF16),
            "w_o": w_o[l].astype(BF16), "ln1_g": vec(ln1_g[l]), "ln1_b": vec(ln1_b[l]),
            "w_gate": w_gate[l].astype(BF16), "w_up": w_up[l].astype(BF16), "w_down": w_down[l].astype(BF16),
            "ln2_g": vec(ln2_g[l]), "ln2_b": vec(ln2_b[l]),
        }
        if l > 0:
            lp["v0"] = vec(rwkv_v0[l - 1])
            lp["v1"] = jnp.pad(rwkv_v1[l - 1], ((0, 0), (0, MV_PAD - MV_LORA))).astype(BF16)
            lp["v2"] = _pad_rows(rwkv_v2[l - 1], 0, MV_PAD).astype(BF16)
        z = _inproj_call(h, _pack_in_proj(w_in[l]))
        prep = _prep_call(z, rot_tabs, lp, h if l > 0 else None, v_first, batch, seq)
        conv, r, lw, k, v, a, b, g, bonus = prep[:9]
        qkv = prep[9:9 + 3 * len(ATTN_GROUPS)]
        if l == 0:
            v_first = prep[-1]
        yb = _scan_call(r, lw, k, v, a, b, bonus, g, vec(rwkv_lnx_g[l]), vec(rwkv_lnx_b[l]), batch, seq)
        outs, lses = [], []
        for gi, (window, dilation) in enumerate(ATTN_GROUPS):
            o, s = _attn_call(*qkv[3 * gi:3 * gi + 3], window, dilation)
            outs.append(o)
            lses.append(s)
        h1, comb_t = _out_call(conv, yb, outs, lses, z, h, lp, w_router_t, rbias, seq)
        comb = jnp.pad(comb_t.T, ((0, 0), (0, LANES - N_EXPERTS)))
        h = _moe_call(h1, comb, lp)
    return h.reshape(batch, seq, d)
```

```python
import functools

import jax
import jax.numpy as jnp
import numpy as np
from jax import lax
from jax.experimental import pallas as pl
from jax.experimental.pallas import tpu as pltpu

F32 = jnp.float32
BF16 = jnp.bfloat16

D_MODEL = 1024
HEAD_DIM = 64
N_HEADS = D_MODEL // HEAD_DIM
LANES = 128
HEADS_PER_VREG = LANES // HEAD_DIM
N_PAIRS = D_MODEL // LANES
DECAY_LORA = 64
AAA_LORA = 64
GATE_LORA = 160
MV_LORA = 32
LORA_PAD = 384
MV_PAD = 128
GN_EPS = 1e-5 * HEAD_DIM
ATTN_GROUPS = ((128, 1), (512, 4), (2048, 16))
ATTN_HEADS = 8
ATTN_GW = ATTN_HEADS * HEAD_DIM
ATTN_W = len(ATTN_GROUPS) * ATTN_GW
ATTN_BLOCK = 128
ATTN_QB = 2
ROT_DIM = 16
ROPE_THETA = 500000.0
N_EXPERTS = 16
EXPERTS_PER_GROUP = 4
D_EXPERT = 512
DEPTH = 2
ALPHA = (2 * DEPTH) ** 0.25
LN_EPS = 1e-5
NEG_INF = -1e30
SCAN_CHUNK = 64
OUT_SPLIT = 2
MOE_TM = 1024
MOE_ROWS = 160

OFF_GATE = 0
OFF_CONV = 3072
OFF_RKV = 6144
OFF_Q = 9216
OFF_K = 10752
OFF_V = 12288
OFF_LORA = 13824
N_PACK = 14208

VMEM_LIMIT = 56 * 1024 * 1024


def _cparams(sem):
    return pltpu.CompilerParams(dimension_semantics=sem, vmem_limit_bytes=VMEM_LIMIT)


def _layer_norm(x, g, b):
    mu = jnp.mean(x, axis=-1, keepdims=True)
    xc = x - mu
    var = jnp.mean(xc * xc, axis=-1, keepdims=True)
    return xc * lax.rsqrt(var + LN_EPS) * g + b


def _sigmoid(x):
    return 1.0 / (1.0 + jnp.exp(-x))


def _dot(a, b):
    return jnp.dot(a, b, preferred_element_type=F32)


def _dot_nt(a, b):
    return lax.dot_general(a, b, (((1,), (1,)), ((), ())), preferred_element_type=F32)


def _dot_tn(a, b):
    return lax.dot_general(a, b, (((0,), (0,)), ((), ())), preferred_element_type=F32)


def _ln_kernel(x_ref, g_ref, b_ref, o_ref):
    o_ref[...] = _layer_norm(x_ref[...], g_ref[...], b_ref[...])


def _ln_call(x, g, b, tm=512):
    t, d = x.shape
    return pl.pallas_call(
        _ln_kernel,
        grid=(t // tm,),
        in_specs=[pl.BlockSpec((tm, d), lambda i: (i, 0)),
                  pl.BlockSpec((1, d), lambda i: (0, 0)),
                  pl.BlockSpec((1, d), lambda i: (0, 0))],
        out_specs=pl.BlockSpec((tm, d), lambda i: (i, 0)),
        out_shape=jax.ShapeDtypeStruct((t, d), F32),
        compiler_params=_cparams(("parallel",)),
        name="ln_in",
    )(x, g.reshape(1, d), b.reshape(1, d))


def _inproj_kernel(x_ref, w_ref, o_ref):
    o_ref[...] = _dot(x_ref[...].astype(BF16), w_ref[...]).astype(o_ref.dtype)


def _inproj_call(h, w, tm=512, tn=4736):
    t, d = h.shape
    n = w.shape[1]
    return pl.pallas_call(
        _inproj_kernel,
        grid=(n // tn, t // tm),
        in_specs=[pl.BlockSpec((tm, d), lambda j, i: (i, 0)),
                  pl.BlockSpec((d, tn), lambda j, i: (0, j))],
        out_specs=pl.BlockSpec((tm, tn), lambda j, i: (i, j)),
        out_shape=jax.ShapeDtypeStruct((t, n), BF16),
        compiler_params=_cparams(("parallel", "parallel")),
        name="in_proj",
    )(h, w)


def _shift_rows(x, carry8, n):
    rolled = pltpu.roll(x, n, axis=0)
    cr = pltpu.roll(carry8, n, axis=0)
    row = lax.broadcasted_iota(jnp.int32, (8, x.shape[1]), 0)
    top = jnp.where(row < n, cr, rolled[:8])
    return jnp.concatenate([top, rolled[8:]], axis=0)


def _prep_kernel(has_vres, tiles_per_seq,
                 zc_ref, zr_ref, zl_ref, zq_ref, zk_ref, zv_ref, rc_ref, rm_ref, rp_ref,
                 convw_ref, mu_r_ref, mu_l_ref, w0_ref, w2_ref, a0_ref, a2_ref, g2_ref,
                 kk_ref, ka_ref, rk_ref, seg_ref, *rest):
    if has_vres:
        u_ref, vf_ref, v0_ref, v1_ref, v2_ref = rest[:5]
        rest = rest[5:]
    conv_o, r_o, lw_o, k_o, v_o, a_o, b_o, g_o, bonus_o = rest[:9]
    n_g = len(ATTN_GROUPS)
    qkv_o = rest[9:9 + 3 * n_g]
    rest = rest[9 + 3 * n_g:]
    if not has_vres:
        vf_o = rest[0]
        rest = rest[1:]
    c_conv, c_zr, c_zl, c_rows = rest
    i = pl.program_id(0)

    @pl.when(i % tiles_per_seq == 0)
    def _():
        c_conv[...] = jnp.zeros_like(c_conv)
        c_zr[...] = jnp.zeros_like(c_zr)
        c_zl[...] = jnp.zeros_like(c_zl)

    tm = zc_ref.shape[0]
    d = D_MODEL
    zc = zc_ref[...].astype(F32)
    ch = zc[:, d:2 * d] * zc[:, 2 * d:]
    carry = c_conv[...]
    conv = (convw_ref[0:1, :] * _shift_rows(ch, carry, 2)
            + convw_ref[1:2, :] * _shift_rows(ch, carry, 1)
            + convw_ref[2:3, :] * ch)
    conv_o[...] = (zc[:, :d] * conv).astype(conv_o.dtype)
    c_conv[...] = ch[tm - 8:, :]

    zr = zr_ref[...].astype(F32)
    zl = zl_ref[...].astype(F32)
    zr_m = zr + (_shift_rows(zr, c_zr[...], 1) - zr) * mu_r_ref[...]
    zl_m = zl + (_shift_rows(zl, c_zl[...], 1) - zl) * mu_l_ref[...]
    c_zr[...] = zr[tm - 8:, :]
    c_zl[...] = zl[tm - 8:, :]
    r = zr_m[:, :d]
    k = zr_m[:, d:2 * d]
    v = zr_m[:, 2 * d:]
    x = w0_ref[...] + _dot(jnp.tanh(zl_m).astype(BF16), w2_ref[...])
    nx = -x
    softplus = jnp.maximum(nx, 0.0) + jnp.log(1.0 + jnp.exp(-jnp.abs(nx)))
    lw_o[...] = -jnp.exp(-softplus - 0.5)
    if has_vres:
        lo = _dot(u_ref[...].astype(BF16), v1_ref[...])
        gate_v = _sigmoid(v0_ref[...] + _dot(lo.astype(BF16), v2_ref[...]))
        v = v + (vf_ref[...].astype(F32) - v) * gate_v
    else:
        vf_o[...] = v.astype(vf_o.dtype)
    zl_b = zl_m.astype(BF16)
    a_gate = _sigmoid(a0_ref[...] + _dot(zl_b, a2_ref[...]))
    g_o[...] = _dot(_sigmoid(zl_m).astype(BF16), g2_ref[...]).astype(g_o.dtype)
    seg = seg_ref[...]
    kk = k * kk_ref[...]
    ss = _dot((kk * kk).astype(BF16), seg)
    kk = kk * lax.rsqrt(jnp.maximum(ss, 1e-24))
    k2 = k * (1.0 + (a_gate - 1.0) * ka_ref[...])
    bonus = _dot((r * k2 * rk_ref[...]).astype(BF16), seg) * v
    r_o[...] = r.astype(r_o.dtype)
    k_o[...] = k2.astype(k_o.dtype)
    v_o[...] = v.astype(v_o.dtype)
    a_o[...] = (-kk).astype(a_o.dtype)
    b_o[...] = (kk * a_gate).astype(b_o.dtype)
    bonus_o[...] = bonus.astype(bonus_o.dtype)

    reps = ATTN_W // LANES
    rc = jnp.concatenate([rc_ref[...]] * reps, axis=1)
    rm = jnp.concatenate([rm_ref[...]] * reps, axis=1)
    rp = jnp.concatenate([rp_ref[...]] * reps, axis=1)
    half = ROT_DIM // 2
    for which, (src, scale) in enumerate(((zq_ref, HEAD_DIM ** -0.5), (zk_ref, 1.0), (zv_ref, None))):
        t = src[...].astype(F32)
        if scale is not None:
            t = (t * rc + pltpu.roll(t, ATTN_W - half, axis=1) * rm + pltpu.roll(t, half, axis=1) * rp) * scale
        for gi, (_, dil) in enumerate(ATTN_GROUPS):
            dst = qkv_o[3 * gi + which]
            tg = t[:, gi * ATTN_GW:(gi + 1) * ATTN_GW]
            if dil == 1:
                dst[0] = tg.astype(dst.dtype)
            else:
                for j in range(ATTN_GW // LANES):
                    c_rows[j] = tg[:, j * LANES:(j + 1) * LANES]
                for res in range(dil):
                    for j in range(ATTN_GW // LANES):
                        dst[res, :, j * LANES:(j + 1) * LANES] = (
                            c_rows[j, pl.ds(res, tm // dil, stride=dil), :].astype(dst.dtype))


def _prep_call(z, rot_tabs, lp, u, v_first, batch, seq, tm=256):
    t = z.shape[0]
    d = D_MODEL
    has_vres = v_first is not None
    tps = seq // tm
    row = lambda w, j: pl.BlockSpec((tm, w), lambda i, j=j: (i, j))
    full = lambda a: pl.BlockSpec(a.shape, lambda i: (0,) * a.ndim)
    consts = [lp["conv_w"], lp["mu_r"], lp["mu_l"], lp["w0"], lp["w2"], lp["a0"], lp["a2"], lp["g2"],
              lp["k_k"], lp["k_a"], lp["r_k"], lp["seg"]]
    args = [z, z, z, z, z, z, *rot_tabs, *consts]
    in_specs = [row(3 * d, OFF_CONV // (3 * d)), row(3 * d, OFF_RKV // (3 * d)),
                row(LORA_PAD, OFF_LORA // LORA_PAD), row(ATTN_W, OFF_Q // ATTN_W),
                row(ATTN_W, OFF_K // ATTN_W), row(ATTN_W, OFF_V // ATTN_W),
                row(LANES, 0), row(LANES, 0), row(LANES, 0)] + [full(a) for a in consts]
    if has_vres:
        extra = [u, v_first, lp["v0"], lp["v1"], lp["v2"]]
        args += extra
        in_specs += [row(d, 0), row(d, 0)] + [full(a) for a in extra[2:]]
    out_w = [(d, BF16), (d, BF16), (d, F32), (d, BF16), (d, BF16), (d, BF16), (d, BF16), (d, BF16), (d, BF16)]
    out_specs = [row(w, 0) for w, _ in out_w]
    out_shape = [jax.ShapeDtypeStruct((t, w), dt) for w, dt in out_w]
    for _, dil in ATTN_GROUPS:
        for _ in range(3):
            out_specs.append(pl.BlockSpec((None, dil, tm // dil, ATTN_GW),
                                          lambda i: (i // tps, 0, i % tps, 0)))
            out_shape.append(jax.ShapeDtypeStruct((batch, dil, seq // dil, ATTN_GW), BF16))
    if not has_vres:
        out_specs.append(row(d, 0))
        out_shape.append(jax.ShapeDtypeStruct((t, d), BF16))
    return pl.pallas_call(
        functools.partial(_prep_kernel, has_vres, tps),
        grid=(t // tm,),
        in_specs=in_specs,
        out_specs=out_specs,
        out_shape=out_shape,
        scratch_shapes=[pltpu.VMEM((8, d), F32), pltpu.VMEM((8, 3 * d), F32), pltpu.VMEM((8, LORA_PAD), F32),
                        pltpu.VMEM((ATTN_GW // LANES, tm, LANES), F32)],
        compiler_params=_cparams(("arbitrary",)),
        name="mixer_prep",
    )(*args)


def _scan_kernel(r_ref, lw_ref, k_ref, v_ref, a_ref, b_ref, bonus_ref, g_ref, lng_ref, lnb_ref,
                 o_ref, h_ref):
    c = pl.program_id(1)

    @pl.when(c == 0)
    def _():
        h_ref[...] = jnp.zeros_like(h_ref)

    C = SCAN_CHUNK
    n2 = HEADS_PER_VREG * C
    ri = lax.broadcasted_iota(jnp.int32, (C, C), 0)
    ci = lax.broadcasted_iota(jnp.int32, (C, C), 1)
    tri = (ri >= ci).astype(BF16)
    lw_all = lw_ref[...]
    lw_hi = lw_all.astype(BF16)
    rem = lw_all - lw_hi.astype(F32)
    lw_mid = rem.astype(BF16)
    lw_lo = (rem - lw_mid.astype(F32)).astype(BF16)
    cum_all = _dot(tri, lw_hi) + (_dot(tri, lw_mid) + _dot(tri, lw_lo))
    lane = lax.broadcasted_iota(jnp.int32, (1, LANES), 1)
    head_masks = [((lane // HEAD_DIM) == hh).astype(F32) for hh in range(HEADS_PER_VREG)]
    r2 = lax.broadcasted_iota(jnp.int32, (n2, n2), 0)
    c2 = lax.broadcasted_iota(jnp.int32, (n2, n2), 1)
    same = (r2 // C) == (c2 // C)
    strict = same & ((r2 % C) > (c2 % C))
    incl = same & ((r2 % C) >= (c2 % C))
    eye = (r2 == c2).astype(F32)
    lr = lax.broadcasted_iota(jnp.int32, (LANES, LANES), 0)
    lc = lax.broadcasted_iota(jnp.int32, (LANES, LANES), 1)
    seg_mean = jnp.where((lr // HEAD_DIM) == (lc // HEAD_DIM), 1.0 / HEAD_DIM, 0.0).astype(BF16)

    def stack(x):
        return jnp.concatenate([x * m for m in head_masks], axis=0).astype(BF16)

    pairs = range(N_PAIRS)
    sls = [slice(p * LANES, (p + 1) * LANES) for p in pairs]
    lhs, rhs, bk, v2, dec = [], [], [], [], []
    for sl in sls:
        lw = lw_ref[:, sl]
        cum = cum_all[:, sl]
        tot = cum[C - 1:C, :]
        e_bwd = jnp.exp(-cum)
        e_rest = jnp.exp(tot - cum)
        r = r_ref[:, sl].astype(F32)
        k = k_ref[:, sl].astype(F32)
        a = a_ref[:, sl].astype(F32)
        b = b_ref[:, sl].astype(F32)
        lhs.append(jnp.concatenate([stack(a * jnp.exp(cum - lw)), stack(r * jnp.exp(cum))], axis=0))
        rhs.append(jnp.concatenate([stack(b * e_bwd), stack(k * e_bwd)], axis=0))
        bk.append(jnp.concatenate([stack(b * e_rest), stack(k * e_rest)], axis=0))
        v2.append(stack(v_ref[:, sl].astype(F32)))
        dec.append(jnp.exp(tot))
    amat = [_dot_nt(lhs[p], rhs[p]) for p in pairs]
    a_ab = [jnp.where(strict, m[:n2, :n2], 0.0) for m in amat]
    a_ak = [jnp.where(strict, m[:n2, n2:], 0.0).astype(BF16) for m in amat]
    a_r = [jnp.concatenate([jnp.where(incl, m[n2:, :n2], 0.0), jnp.where(incl, m[n2:, n2:], 0.0)],
                           axis=1).astype(BF16) for m in amat]
    tinv = [eye + m for m in a_ab]
    ab = [m.astype(BF16) for m in a_ab]
    pw = [_dot(m, m) for m in ab]
    levels = int(np.log2(C)) - 1
    for lvl in range(levels):
        pb = [m.astype(BF16) for m in pw]
        if lvl < levels - 1:
            res = [_dot(pb[p], jnp.concatenate([tinv[p].astype(BF16), pb[p]], axis=1)) for p in pairs]
            tinv = [tinv[p] + res[p][:, :n2] for p in pairs]
            pw = [m[:, n2:] for m in res]
        else:
            tinv = [tinv[p] + _dot(pb[p], tinv[p].astype(BF16)) for p in pairs]
    ht = [h_ref[p] for p in pairs]
    ah = [_dot_nt(lhs[p], ht[p].astype(BF16)) for p in pairs]
    w = [ah[p][:n2] + _dot(a_ak[p], v2[p]) for p in pairs]
    u = [_dot(tinv[p].astype(BF16), w[p].astype(BF16)) for p in pairs]
    uv = [jnp.concatenate([u[p].astype(BF16), v2[p]], axis=0) for p in pairs]
    y2 = [ah[p][n2:] + _dot(a_r[p], uv[p]) for p in pairs]
    for p in pairs:
        h_ref[p] = ht[p] * dec[p] + _dot_tn(uv[p], bk[p])
    ys = []
    for m in y2:
        y = m[:C]
        for hh in range(1, HEADS_PER_VREG):
            y = y + m[hh * C:(hh + 1) * C]
        ys.append(y)
    y = jnp.concatenate(ys, axis=0)
    y_hi = y.astype(BF16)
    ym = _dot(y_hi, seg_mean) + _dot((y - y_hi.astype(F32)).astype(BF16), seg_mean)
    yc = y - ym
    yv = _dot((yc * yc).astype(BF16), seg_mean)
    yn = yc * lax.rsqrt(yv + GN_EPS)
    for p, sl in enumerate(sls):
        out = yn[p * C:(p + 1) * C] * lng_ref[:, sl] + lnb_ref[:, sl]
        out = (out + bonus_ref[:, sl].astype(F32)) * g_ref[:, sl].astype(F32)
        o_ref[:, sl] = out.astype(o_ref.dtype)


def _scan_call(r, lw, k, v, a, b, bonus, g, lng, lnb, batch, seq):
    t, d = r.shape
    C = SCAN_CHUNK
    nc = seq // C
    blk = pl.BlockSpec((C, d), lambda bi, ci: (bi * nc + ci, 0))
    full = pl.BlockSpec((1, d), lambda bi, ci: (0, 0))
    return pl.pallas_call(
        _scan_kernel,
        grid=(batch, nc),
        in_specs=[blk] * 8 + [full, full],
        out_specs=blk,
        out_shape=jax.ShapeDtypeStruct((t, d), BF16),
        scratch_shapes=[pltpu.VMEM((N_PAIRS, LANES, LANES), F32)],
        compiler_params=_cparams(("parallel", "arbitrary")),
        name="wkv7_scan",
    )(r, lw, k, v, a, b, bonus, g, lng, lnb)


def _attn_kernel(span, q_ref, kp_ref, kc_ref, vp_ref, vc_ref, o_ref, lse_ref):
    n = pl.program_id(2)
    Q = ATTN_BLOCK
    qi = lax.broadcasted_iota(jnp.int32, (Q, 2 * Q), 0)
    kj = lax.broadcasted_iota(jnp.int32, (Q, 2 * Q), 1)
    dist = Q + qi - kj
    band = (dist >= 0) & (dist <= span)
    first = band & ((n > 0) | (kj >= Q))
    lane = lax.broadcasted_iota(jnp.int32, (1, LANES), 1)
    masks = [(lane // HEAD_DIM) == hh for hh in range(HEADS_PER_VREG)]
    sls = [slice(p * LANES, (p + 1) * LANES) for p in range(ATTN_GW // LANES)]
    qbs = range(ATTN_QB)

    def keys(prev_ref, cur_ref, j, sl):
        lo = prev_ref[:, sl] if j == 0 else cur_ref[(j - 1) * Q:j * Q, sl]
        return jnp.concatenate([lo, cur_ref[j * Q:(j + 1) * Q, sl]], axis=0)

    chains = [(j, sl, m) for j in qbs for sl in sls for m in masks]
    scores = []
    for j, sl, m in chains:
        q = q_ref[j * Q:(j + 1) * Q, sl]
        scores.append(_dot_nt(jnp.where(m, q, jnp.zeros_like(q)), keys(kp_ref, kc_ref, j, sl)))
    probs, lses = [], []
    for (j, _, _), s in zip(chains, scores):
        s = jnp.where(first if j == 0 else band, s, NEG_INF)
        mx = jnp.max(s, axis=-1, keepdims=True)
        e = jnp.exp(s - mx)
        den = jnp.sum(e, axis=-1, keepdims=True)
        probs.append((e / den).astype(BF16))
        lses.append(mx + jnp.log(den))
    nh = len(masks)
    for c0 in range(0, len(chains), nh):
        j, sl, _ = chains[c0]
        vv = keys(vp_ref, vc_ref, j, sl)
        out = None
        lse = None
        for hh, m in enumerate(masks):
            o = _dot(probs[c0 + hh], jnp.where(m, vv, jnp.zeros_like(vv)))
            l = lses[c0 + hh]
            out = o if out is None else out + o
            lse = jnp.broadcast_to(l, (Q, LANES)) if lse is None else jnp.where(m, l, lse)
        o_ref[j * Q:(j + 1) * Q, sl] = out.astype(o_ref.dtype)
        lse_ref[j * Q:(j + 1) * Q, sl] = lse


def _attn_call(q, k, v, window, dilation):
    batch, _, L, _ = q.shape
    Q = ATTN_BLOCK
    nb = L // (ATTN_QB * Q)
    cur = pl.BlockSpec((None, None, ATTN_QB * Q, ATTN_GW), lambda b, r, n: (b, r, n, 0))
    prev = pl.BlockSpec((None, None, Q, ATTN_GW),
                        lambda b, r, n: (b, r, jnp.maximum(ATTN_QB * n - 1, 0), 0))
    return pl.pallas_call(
        functools.partial(_attn_kernel, window // dilation),
        grid=(batch, dilation, nb),
        in_specs=[cur, prev, cur, prev, cur],
        out_specs=[cur, cur],
        out_shape=[jax.ShapeDtypeStruct(q.shape, BF16), jax.ShapeDtypeStruct(q.shape, F32)],
        compiler_params=_cparams(("parallel", "parallel", "arbitrary")),
        name=f"dilated_attn_d{dilation}",
    )(q, k, k, v, v)


def _route(logits, bias):
    mxl = jnp.max(logits, axis=0, keepdims=True)
    ex = jnp.exp(logits - mxl)
    probs = ex / jnp.sum(ex, axis=0, keepdims=True)
    sel = probs + bias
    rows = [sel[e:e + 1, :] for e in range(N_EXPERTS)]
    n_groups = N_EXPERTS // EXPERTS_PER_GROUP
    scores = []
    for gi in range(n_groups):
        a, b, c, dd = rows[EXPERTS_PER_GROUP * gi:EXPERTS_PER_GROUP * (gi + 1)]
        hi1, lo1 = jnp.maximum(a, b), jnp.minimum(a, b)
        hi2, lo2 = jnp.maximum(c, dd), jnp.minimum(c, dd)
        scores.append(jnp.maximum(hi1, hi2) + jnp.maximum(jnp.minimum(hi1, hi2), jnp.maximum(lo1, lo2)))
    best_score = scores[0]
    best = jnp.zeros_like(scores[0], dtype=jnp.int32)
    for gi in range(1, n_groups):
        better = scores[gi] > best_score
        best = jnp.where(better, gi, best)
        best_score = jnp.where(better, scores[gi], best_score)
    eid = lax.broadcasted_iota(jnp.int32, sel.shape, 0)
    masked = jnp.where((eid // EXPERTS_PER_GROUP) == best, sel, -jnp.inf)
    m1 = jnp.max(masked, axis=0, keepdims=True)
    i1 = jnp.min(jnp.where(masked == m1, eid, N_EXPERTS), axis=0, keepdims=True)
    masked2 = jnp.where(eid == i1, -jnp.inf, masked)
    m2 = jnp.max(masked2, axis=0, keepdims=True)
    i2 = jnp.min(jnp.where(masked2 == m2, eid, N_EXPERTS), axis=0, keepdims=True)
    p1 = jnp.sum(jnp.where(eid == i1, probs, 0.0), axis=0, keepdims=True)
    p2 = jnp.sum(jnp.where(eid == i2, probs, 0.0), axis=0, keepdims=True)
    tot = p1 + p2
    return jnp.where(eid == i1, p1 / tot, 0.0) + jnp.where(eid == i2, p2 / tot, 0.0)


def _out_kernel(conv_ref, yb_ref, o0_ref, o1_ref, o2_ref, l0_ref, l1_ref, l2_ref, zg_ref, h_ref,
                pa_ref, pb_ref, pc_ref, wo_ref, g_ref, b_ref, wr_ref, rb_ref, h1_ref, comb_ref, cnt_ref, sc_ref):
    d = D_MODEL
    tm = h_ref.shape[0]
    hm = tm // OUT_SPLIT
    nl = ATTN_GW // LANES
    dils = [dil for _, dil in ATTN_GROUPS]
    slot = iter(range(sc_ref.shape[0]))

    def token_order(ref, dil, s):
        n = hm // dil
        if dil == 1:
            return ref[0, s * hm:(s + 1) * hm, :].astype(F32)
        buf = next(slot)
        for res in range(dil):
            blk = ref[res, s * n:(s + 1) * n, :].astype(F32)
            for j in range(nl):
                sc_ref[buf, j, pl.ds(res, n, stride=dil), :] = blk[:, j * LANES:(j + 1) * LANES]
        return jnp.concatenate([sc_ref[buf, j] for j in range(nl)], axis=1)

    subs = range(OUT_SPLIT)
    rows = [slice(s * hm, (s + 1) * hm) for s in subs]
    ycs = []
    for s in subs:
        l0, l1, l2 = [token_order(r, dl, s) for r, dl in zip((l0_ref, l1_ref, l2_ref), dils)]
        mx = jnp.maximum(jnp.maximum(l0, l1), l2)
        e0, e1, e2 = jnp.exp(l0 - mx), jnp.exp(l1 - mx), jnp.exp(l2 - mx)
        yc = e0 * token_order(o0_ref, dils[0], s)
        yc = yc + e1 * token_order(o1_ref, dils[1], s)
        yc = (yc + e2 * token_order(o2_ref, dils[2], s)) / (e0 + e1 + e2)
        ycs.append(yc.astype(BF16))
    ya = [_dot(conv_ref[r, :], pa_ref[...]) for r in rows]
    yb = [_dot(yb_ref[r, :], pb_ref[...]) for r in rows]
    yc = [_dot(ycs[s], pc_ref[...]) for s in subs]
    mixes = []
    for s, r in enumerate(rows):
        ga = _sigmoid(zg_ref[r, 0:d].astype(F32))
        gb = _sigmoid(zg_ref[r, d:2 * d].astype(F32))
        gc = _sigmoid(zg_ref[r, 2 * d:3 * d].astype(F32))
        mixes.append((ga * ya[s] + gb * yb[s] + gc * yc[s]).astype(BF16))
    outs = [_dot(m, wo_ref[...]) for m in mixes]
    h1s = []
    for s, r in enumerate(rows):
        h1 = _layer_norm(ALPHA * h_ref[r, :] + outs[s], g_ref[...], b_ref[...])
        h1_ref[r, :] = h1
        h1s.append(h1)
    logits = [lax.dot_general(wr_ref[...], h1, (((1,), (1,)), ((), ())), preferred_element_type=F32,
                              precision=lax.Precision.HIGHEST) for h1 in h1s]
    count = jnp.zeros((N_EXPERTS, 1), F32)
    for s, r in enumerate(rows):
        comb = _route(logits[s], rb_ref[...])
        comb_ref[:, r] = comb
        count = count + jnp.sum((comb > 0.0).astype(F32), axis=1, keepdims=True)
    cnt_ref[...] = jnp.broadcast_to(count, cnt_ref.shape)


def _out_call(conv, yb, outs, lses, z, h, lp, w_router_t, router_bias, seq, tm=512):
    t, d = h.shape
    tps = seq // tm
    row = lambda w, j=0: pl.BlockSpec((tm, w), lambda i, j=j: (i, j))
    full = lambda a: pl.BlockSpec(a.shape, lambda i: (0,) * a.ndim)
    res = [pl.BlockSpec((None, dil, tm // dil, ATTN_GW), lambda i: (i // tps, 0, i % tps, 0))
           for _, dil in ATTN_GROUPS]
    consts = [lp["p_a"], lp["p_b"], lp["p_c"], lp["w_o"], lp["ln1_g"], lp["ln1_b"], w_router_t, router_bias]
    n_relayout = 2 * OUT_SPLIT * sum(1 for _, dil in ATTN_GROUPS if dil > 1)
    return pl.pallas_call(
        _out_kernel,
        grid=(t // tm,),
        in_specs=[row(d), row(d)] + res + res + [row(3 * d, OFF_GATE), row(d)]
                 + [full(a) for a in consts],
        out_specs=[row(d), pl.BlockSpec((N_EXPERTS, tm), lambda i: (0, i)),
                   pl.BlockSpec((None, N_EXPERTS, LANES), lambda i: (i, 0, 0))],
        out_shape=[jax.ShapeDtypeStruct((t, d), F32), jax.ShapeDtypeStruct((N_EXPERTS, t), F32),
                   jax.ShapeDtypeStruct((t // tm, N_EXPERTS, LANES), F32)],
        scratch_shapes=[pltpu.VMEM((n_relayout, ATTN_GW // LANES, tm // OUT_SPLIT, LANES), F32)],
        compiler_params=_cparams(("parallel",)),
        name="mixer_out",
    )(conv, yb, *outs, *lses, z, h, *consts)


def _moe_kernel(cnt_ref, x_ref, comb_ref, wg_ref, wu_ref, wd_ref, g_ref, b_ref, o_ref,
                xb_ref, acc_ref, rank_ref):
    i = pl.program_id(0)
    e = pl.program_id(1)
    tm = x_ref.shape[0]
    R = MOE_ROWS

    @pl.when(e == 0)
    def _():
        xb_ref[...] = x_ref[...].astype(BF16)
        acc_ref[...] = jnp.zeros_like(acc_ref)
        sel = (comb_ref[...] > 0.0).astype(BF16)
        tri = (lax.broadcasted_iota(jnp.int32, (tm, tm), 0)
               <= lax.broadcasted_iota(jnp.int32, (tm, tm), 1)).astype(BF16)
        rank_ref[...] = _dot(sel, tri) - 1.0

    comb_e = comb_ref[pl.ds(e, 1), :]
    rank_e = jnp.where(comb_e > 0.0, rank_ref[pl.ds(e, 1), :], -1.0)
    slot = lax.broadcasted_iota(jnp.int32, (R, tm), 0).astype(F32)

    def block(blk, carry):
        hit = rank_e == slot + (blk * R).astype(F32)
        onehot = hit.astype(BF16)
        xg = _dot(onehot, xb_ref[...]).astype(BF16)
        cw = jnp.sum(jnp.where(hit, comb_e, 0.0), axis=1, keepdims=True)
        gate = _dot(xg, wg_ref[...])
        up = _dot(xg, wu_ref[...])
        hid = gate * _sigmoid(gate) * up * cw
        y = _dot(hid.astype(BF16), wd_ref[...]).astype(BF16)
        acc_ref[...] += _dot_tn(onehot, y)
        return carry

    lax.fori_loop(0, (cnt_ref[i, e] + R - 1) // R, block, 0)

    @pl.when(e == N_EXPERTS - 1)
    def _():
        o_ref[...] = _layer_norm(ALPHA * x_ref[...] + acc_ref[...], g_ref[...], b_ref[...])


def _moe_call(h1, comb_t, counts, lp, tm=1024):
    t, d = h1.shape
    wsp = lambda s: pl.BlockSpec((None,) + s, lambda i, e, c: (e, 0, 0))
    vec = pl.BlockSpec((1, d), lambda i, e, c: (0, 0))
    return pl.pallas_call(
        _moe_kernel,
        grid_spec=pltpu.PrefetchScalarGridSpec(
            num_scalar_prefetch=1,
            grid=(t // tm, N_EXPERTS),
            in_specs=[pl.BlockSpec((tm, d), lambda i, e, c: (i, 0)),
                      pl.BlockSpec((N_EXPERTS, tm), lambda i, e, c: (0, i)),
                      wsp((d, D_EXPERT)), wsp((d, D_EXPERT)), wsp((D_EXPERT, d)), vec, vec],
            out_specs=pl.BlockSpec((tm, d), lambda i, e, c: (i, 0)),
            scratch_shapes=[pltpu.VMEM((tm, d), BF16), pltpu.VMEM((tm, d), F32),
                            pltpu.VMEM((N_EXPERTS, tm), F32)]),
        out_shape=jax.ShapeDtypeStruct((t, d), F32),
        compiler_params=_cparams(("parallel", "arbitrary")),
        name="moe",
    )(counts, h1, comb_t, lp["w_gate"], lp["w_up"], lp["w_down"], lp["ln2_g"], lp["ln2_b"])


def _pad_rows(w, start, total):
    return jnp.zeros((total, w.shape[1]), w.dtype).at[start:start + w.shape[0]].set(w)


def _rotary_tables(positions):
    half = ROT_DIM // 2
    inv_freq = ROPE_THETA ** (-jnp.arange(0, ROT_DIM, 2, dtype=F32) / ROT_DIM)
    ang = positions.astype(F32).reshape(-1, 1) * inv_freq
    cos, sin = jnp.cos(ang), jnp.sin(ang)
    t = ang.shape[0]
    pad = HEAD_DIM - ROT_DIM
    rc = jnp.concatenate([cos, cos, jnp.ones((t, pad), F32)], axis=1)
    rm = jnp.concatenate([-sin, jnp.zeros((t, half + pad), F32)], axis=1)
    rp = jnp.concatenate([jnp.zeros((t, half), F32), sin, jnp.zeros((t, pad), F32)], axis=1)
    return tuple(jnp.tile(x, (1, HEADS_PER_VREG)) for x in (rc, rm, rp))


def _pack_in_proj(w):
    d = D_MODEL
    gate, conv, rwkv, attn = (w[:, :3 * d], w[:, 3 * d:6 * d], w[:, 6 * d:6 * d + 3 * d + 288],
                              w[:, 6 * d + 3 * d + 288:])
    lora = jnp.pad(rwkv[:, 3 * d:], ((0, 0), (0, LORA_PAD - 288)))
    return jnp.concatenate([gate, conv, rwkv[:, :3 * d], attn, lora], axis=1).astype(BF16)


def kernel(x, positions, ln_in_g, ln_in_b, w_in, conv_w, rwkv_mu, rwkv_w0, rwkv_w2, rwkv_a0, rwkv_a2, rwkv_g2, rwkv_v0, rwkv_v1, rwkv_v2, rwkv_k_k, rwkv_k_a, rwkv_r_k, rwkv_lnx_g, rwkv_lnx_b, p_a, p_b, p_c, w_o, ln1_g, ln1_b, w_router, router_bias, w_gate, w_up, w_down, ln2_g, ln2_b):
    batch, seq, d = x.shape
    t = batch * seq
    depth = w_in.shape[0]
    assert d == D_MODEL and seq % (ATTN_QB * ATTN_BLOCK * ATTN_GROUPS[-1][1]) == 0
    rot_tabs = _rotary_tables(positions)
    hid = lax.broadcasted_iota(jnp.int32, (d, d), 0) // HEAD_DIM
    seg = (hid == hid.T).astype(BF16)
    w_router_t = w_router.T.astype(F32)
    rbias = router_bias.reshape(N_EXPERTS, 1).astype(F32)
    vec = lambda a: a.reshape(1, -1).astype(F32)

    h = _ln_call(x.reshape(t, d), ln_in_g, ln_in_b)
    v_first = None
    for l in range(depth):
        lp = {
            "conv_w": conv_w[l],
            "mu_r": vec(rwkv_mu[l][:3 * d]),
            "mu_l": vec(jnp.pad(rwkv_mu[l][3 * d:], (0, LORA_PAD - 288))),
            "w0": vec(rwkv_w0[l]),
            "w2": _pad_rows(rwkv_w2[l], 0, LORA_PAD).astype(BF16),
            "a0": vec(rwkv_a0[l]),
            "a2": _pad_rows(rwkv_a2[l], DECAY_LORA, LORA_PAD).astype(BF16),
            "g2": _pad_rows(rwkv_g2[l], DECAY_LORA + AAA_LORA, LORA_PAD).astype(BF16),
            "k_k": vec(rwkv_k_k[l]), "k_a": vec(rwkv_k_a[l]), "r_k": vec(rwkv_r_k[l]),
            "seg": seg,
            "p_a": p_a[l].astype(BF16), "p_b": p_b[l].astype(BF16), "p_c": p_c[l].astype(BF16),
            "w_o": w_o[l].astype(BF16), "ln1_g": vec(ln1_g[l]), "ln1_b": vec(ln1_b[l]),
            "w_gate": w_gate[l].astype(BF16), "w_up": w_up[l].astype(BF16), "w_down": w_down[l].astype(BF16),
            "ln2_g": vec(ln2_g[l]), "ln2_b": vec(ln2_b[l]),
        }
        if l > 0:
            lp["v0"] = vec(rwkv_v0[l - 1])
            lp["v1"] = jnp.pad(rwkv_v1[l - 1], ((0, 0), (0, MV_PAD - MV_LORA))).astype(BF16)
            lp["v2"] = _pad_rows(rwkv_v2[l - 1], 0, MV_PAD).astype(BF16)
        z = _inproj_call(h, _pack_in_proj(w_in[l]))
        prep = _prep_call(z, rot_tabs, lp, h if l > 0 else None, v_first, batch, seq)
        conv, r, lw, k, v, a, b, g, bonus = prep[:9]
        qkv = prep[9:9 + 3 * len(ATTN_GROUPS)]
        if l == 0:
            v_first = prep[-1]
        yb = _scan_call(r, lw, k, v, a, b, bonus, g, vec(rwkv_lnx_g[l]), vec(rwkv_lnx_b[l]), batch, seq)
        outs, lses = [], []
        for gi, (window, dilation) in enumerate(ATTN_GROUPS):
            o, s = _attn_call(*qkv[3 * gi:3 * gi + 3], window, dilation)
            outs.append(o)
            lses.append(s)
        h1, comb_t, cnt = _out_call(conv, yb, outs, lses, z, h, lp, w_router_t, rbias, seq)
        counts = cnt[:, :, 0].reshape(t // MOE_TM, -1, N_EXPERTS).sum(axis=1).astype(jnp.int32)
        h = _moe_call(h1, comb_t, counts, lp, MOE_TM)
    return h.reshape(batch, seq, d)
```

```python
import functools

import jax
import jax.numpy as jnp
import numpy as np
from jax import lax
from jax.experimental import pallas as pl
from jax.experimental.pallas import tpu as pltpu

F32 = jnp.float32
BF16 = jnp.bfloat16

D_MODEL = 1024
HEAD_DIM = 64
N_HEADS = D_MODEL // HEAD_DIM
LANES = 128
HEADS_PER_VREG = LANES // HEAD_DIM
N_PAIRS = D_MODEL // LANES
DECAY_LORA = 64
AAA_LORA = 64
GATE_LORA = 160
MV_LORA = 32
LORA_PAD = 384
MV_PAD = 128
GN_EPS = 1e-5 * HEAD_DIM
ATTN_GROUPS = ((128, 1), (512, 4), (2048, 16))
ATTN_HEADS = 8
ATTN_GW = ATTN_HEADS * HEAD_DIM
ATTN_W = len(ATTN_GROUPS) * ATTN_GW
ATTN_BLOCK = 128
ATTN_QB = 2
ROT_DIM = 16
ROPE_THETA = 500000.0
N_EXPERTS = 16
EXPERTS_PER_GROUP = 4
D_EXPERT = 512
DEPTH = 2
ALPHA = (2 * DEPTH) ** 0.25
LN_EPS = 1e-5
NEG_INF = -1e30
SCAN_CHUNK = 64
SCAN_SUB = 4
OUT_SPLIT = 4
MOE_TM = 1024
MOE_ROWS = 160

VMEM_LIMIT = 56 * 1024 * 1024


def _cparams(sem):
    return pltpu.CompilerParams(dimension_semantics=sem, vmem_limit_bytes=VMEM_LIMIT)


def _layer_norm(x, g, b):
    mu = jnp.mean(x, axis=-1, keepdims=True)
    xc = x - mu
    var = jnp.mean(xc * xc, axis=-1, keepdims=True)
    return xc * lax.rsqrt(var + LN_EPS) * g + b


def _sigmoid(x):
    return 1.0 / (1.0 + jnp.exp(-x))


def _dot(a, b):
    return jnp.dot(a, b, preferred_element_type=F32)


def _dot_nt(a, b):
    return lax.dot_general(a, b, (((1,), (1,)), ((), ())), preferred_element_type=F32)


def _dot_tn(a, b):
    return lax.dot_general(a, b, (((0,), (0,)), ((), ())), preferred_element_type=F32)


def _ln_kernel(x_ref, g_ref, b_ref, o_ref):
    o_ref[...] = _layer_norm(x_ref[...], g_ref[...], b_ref[...])


def _ln_call(x, g, b, tm=512):
    t, d = x.shape
    return pl.pallas_call(
        _ln_kernel,
        grid=(t // tm,),
        in_specs=[pl.BlockSpec((tm, d), lambda i: (i, 0)),
                  pl.BlockSpec((1, d), lambda i: (0, 0)),
                  pl.BlockSpec((1, d), lambda i: (0, 0))],
        out_specs=pl.BlockSpec((tm, d), lambda i: (i, 0)),
        out_shape=jax.ShapeDtypeStruct((t, d), F32),
        compiler_params=_cparams(("parallel",)),
        name="ln_in",
    )(x, g.reshape(1, d), b.reshape(1, d))


def _shift_rows(x, carry8, n):
    rolled = pltpu.roll(x, n, axis=0)
    cr = pltpu.roll(carry8, n, axis=0)
    row = lax.broadcasted_iota(jnp.int32, (8, x.shape[1]), 0)
    top = jnp.where(row < n, cr, rolled[:8])
    return jnp.concatenate([top, rolled[8:]], axis=0)


def _store_residue_major(dst, tg, dil, c_rows):
    tm = tg.shape[0]
    if dil == 1:
        dst[0] = tg.astype(dst.dtype)
        return
    nl = ATTN_GW // LANES
    for j in range(nl):
        c_rows[j] = tg[:, j * LANES:(j + 1) * LANES]
    for res in range(dil):
        for j in range(nl):
            dst[res, :, j * LANES:(j + 1) * LANES] = (
                c_rows[j, pl.ds(res, tm // dil, stride=dil), :].astype(dst.dtype))


def _proj_gate_kernel(x_ref, w_ref, gate_o, lora_o):
    acc = _dot(x_ref[...].astype(BF16), w_ref[...])
    d3 = 3 * D_MODEL
    gate_o[...] = _sigmoid(acc[:, :d3]).astype(gate_o.dtype)
    lora_o[...] = acc[:, d3:].astype(lora_o.dtype)


def _proj_conv_kernel(tiles_per_seq, x_ref, w_ref, convw_ref, conv_o, v0_o, carry):
    i = pl.program_id(0)

    @pl.when(i % tiles_per_seq == 0)
    def _():
        carry[...] = jnp.zeros_like(carry)

    acc = _dot(x_ref[...].astype(BF16), w_ref[...])
    tm = acc.shape[0]
    d = D_MODEL
    ch = acc[:, d:2 * d] * acc[:, 2 * d:3 * d]
    cr = carry[...]
    conv = (convw_ref[0:1, :] * _shift_rows(ch, cr, 2)
            + convw_ref[1:2, :] * _shift_rows(ch, cr, 1)
            + convw_ref[2:3, :] * ch)
    conv_o[...] = (acc[:, :d] * conv).astype(conv_o.dtype)
    carry[...] = ch[tm - 8:, :]
    v0_o[0] = acc[:, 3 * d:].astype(v0_o.dtype)


def _proj_rkv_kernel(dil, x_ref, w_ref, zr_o, v_o, c_rows):
    acc = _dot(x_ref[...].astype(BF16), w_ref[...])
    d3 = 3 * D_MODEL
    zr_o[...] = acc[:, :d3].astype(zr_o.dtype)
    _store_residue_major(v_o, acc[:, d3:], dil, c_rows)


def _proj_attn_kernel(dil_v, x_ref, w_ref, rc_ref, rm_ref, rp_ref, *rest):
    n_g = len(ATTN_GROUPS)
    q_o, k_o, v_o, c_rows = rest[:n_g], rest[n_g:2 * n_g], rest[2 * n_g], rest[2 * n_g + 1]
    acc = _dot(x_ref[...].astype(BF16), w_ref[...])
    reps = ATTN_W // LANES
    rc = jnp.concatenate([rc_ref[...]] * reps, axis=1)
    rm = jnp.concatenate([rm_ref[...]] * reps, axis=1)
    rp = jnp.concatenate([rp_ref[...]] * reps, axis=1)
    half = ROT_DIM // 2
    for which, (dsts, scale) in enumerate(((q_o, HEAD_DIM ** -0.5), (k_o, 1.0))):
        t = acc[:, which * ATTN_W:(which + 1) * ATTN_W]
        t = (t * rc + pltpu.roll(t, ATTN_W - half, axis=1) * rm + pltpu.roll(t, half, axis=1) * rp) * scale
        for gi, (_, dil) in enumerate(ATTN_GROUPS):
            _store_residue_major(dsts[gi], t[:, gi * ATTN_GW:(gi + 1) * ATTN_GW], dil, c_rows)
    _store_residue_major(v_o, acc[:, 2 * ATTN_W:], dil_v, c_rows)


def _proj_call(kernel, name, h, w, extra, extra_specs, out_specs, out_shape, scratch, sem, tm):
    t, d = h.shape
    return pl.pallas_call(
        kernel,
        grid=(t // tm,),
        in_specs=[pl.BlockSpec((tm, d), lambda i: (i, 0)), pl.BlockSpec(w.shape, lambda i: (0, 0))] + extra_specs,
        out_specs=out_specs,
        out_shape=out_shape,
        scratch_shapes=scratch,
        compiler_params=_cparams((sem,)),
        name=name,
    )(h, w, *extra)


def _in_proj(h, lp, rot_tabs, batch, seq, tm=512):
    t, d = h.shape
    tps = seq // tm
    row = lambda w: pl.BlockSpec((tm, w), lambda i: (i, 0))
    res_spec = lambda dil: pl.BlockSpec((None, dil, tm // dil, ATTN_GW), lambda i: (i // tps, 0, i % tps, 0))
    res_shape = lambda dil: jax.ShapeDtypeStruct((batch, dil, seq // dil, ATTN_GW), BF16)
    rows_scratch = pltpu.VMEM((ATTN_GW // LANES, tm, LANES), F32)
    dils = [dil for _, dil in ATTN_GROUPS]
    gates, lora = _proj_call(
        _proj_gate_kernel, "proj_gate", h, lp["w_pg"], [], [],
        [row(3 * d), row(LORA_PAD)],
        [jax.ShapeDtypeStruct((t, 3 * d), BF16), jax.ShapeDtypeStruct((t, LORA_PAD), BF16)],
        [], "parallel", tm)
    conv, v0 = _proj_call(
        functools.partial(_proj_conv_kernel, tps), "proj_conv", h, lp["w_pc"], [lp["conv_w"]],
        [pl.BlockSpec(lp["conv_w"].shape, lambda i: (0, 0))],
        [row(d), res_spec(dils[0])],
        [jax.ShapeDtypeStruct((t, d), BF16), res_shape(dils[0])],
        [pltpu.VMEM((8, d), F32)], "arbitrary", tm)
    zr, v1 = _proj_call(
        functools.partial(_proj_rkv_kernel, dils[1]), "proj_rkv", h, lp["w_pr"], [], [],
        [row(3 * d), res_spec(dils[1])],
        [jax.ShapeDtypeStruct((t, 3 * d), BF16), res_shape(dils[1])],
        [rows_scratch], "parallel", tm)
    qkv = _proj_call(
        functools.partial(_proj_attn_kernel, dils[2]), "proj_attn", h, lp["w_pa"], list(rot_tabs),
        [row(LANES)] * 3,
        [res_spec(dl) for dl in dils] * 2 + [res_spec(dils[2])],
        [res_shape(dl) for dl in dils] * 2 + [res_shape(dils[2])],
        [rows_scratch], "parallel", tm)
    n_g = len(ATTN_GROUPS)
    q, k, v2 = qkv[:n_g], qkv[n_g:2 * n_g], qkv[2 * n_g]
    return gates, lora, conv, zr, q, k, [v0, v1, v2]


def _prep_kernel(has_vres, tiles_per_seq,
                 zr_ref, zl_ref, mu_r_ref, mu_l_ref, w0_ref, w2_ref, a0_ref, a2_ref, g2_ref,
                 kk_ref, ka_ref, rk_ref, seg_ref, *rest):
    if has_vres:
        u_ref, vf_ref, v0_ref, v1_ref, v2_ref = rest[:5]
        rest = rest[5:]
    r_o, lw_o, k_o, v_o, a_o, b_o, g_o, bonus_o = rest[:8]
    rest = rest[8:]
    if not has_vres:
        vf_o = rest[0]
        rest = rest[1:]
    c_zr, c_zl = rest
    i = pl.program_id(0)

    @pl.when(i % tiles_per_seq == 0)
    def _():
        c_zr[...] = jnp.zeros_like(c_zr)
        c_zl[...] = jnp.zeros_like(c_zl)

    tm = zr_ref.shape[0]
    d = D_MODEL
    zr = zr_ref[...].astype(F32)
    zl = zl_ref[...].astype(F32)
    zr_m = zr + (_shift_rows(zr, c_zr[...], 1) - zr) * mu_r_ref[...]
    zl_m = zl + (_shift_rows(zl, c_zl[...], 1) - zl) * mu_l_ref[...]
    c_zr[...] = zr[tm - 8:, :]
    c_zl[...] = zl[tm - 8:, :]
    r = zr_m[:, :d]
    k = zr_m[:, d:2 * d]
    v = zr_m[:, 2 * d:]
    x = w0_ref[...] + _dot(jnp.tanh(zl_m).astype(BF16), w2_ref[...])
    nx = -x
    softplus = jnp.maximum(nx, 0.0) + jnp.log(1.0 + jnp.exp(-jnp.abs(nx)))
    lw_o[...] = -jnp.exp(-softplus - 0.5)
    if has_vres:
        lo = _dot(u_ref[...].astype(BF16), v1_ref[...])
        gate_v = _sigmoid(v0_ref[...] + _dot(lo.astype(BF16), v2_ref[...]))
        v = v + (vf_ref[...].astype(F32) - v) * gate_v
    else:
        vf_o[...] = v.astype(vf_o.dtype)
    zl_b = zl_m.astype(BF16)
    a_gate = _sigmoid(a0_ref[...] + _dot(zl_b, a2_ref[...]))
    g_o[...] = _dot(_sigmoid(zl_m).astype(BF16), g2_ref[...]).astype(g_o.dtype)
    seg = seg_ref[...]
    kk = k * kk_ref[...]
    ss = _dot((kk * kk).astype(BF16), seg)
    kk = kk * lax.rsqrt(jnp.maximum(ss, 1e-24))
    k2 = k * (1.0 + (a_gate - 1.0) * ka_ref[...])
    bonus = _dot((r * k2 * rk_ref[...]).astype(BF16), seg) * v
    r_o[...] = r.astype(r_o.dtype)
    k_o[...] = k2.astype(k_o.dtype)
    v_o[...] = v.astype(v_o.dtype)
    a_o[...] = (-kk).astype(a_o.dtype)
    b_o[...] = (kk * a_gate).astype(b_o.dtype)
    bonus_o[...] = bonus.astype(bonus_o.dtype)


def _prep_call(zr, zl, lp, u, v_first, seq, tm=256):
    t = zr.shape[0]
    d = D_MODEL
    has_vres = v_first is not None
    row = lambda w: pl.BlockSpec((tm, w), lambda i: (i, 0))
    full = lambda a: pl.BlockSpec(a.shape, lambda i: (0,) * a.ndim)
    consts = [lp["mu_r"], lp["mu_l"], lp["w0"], lp["w2"], lp["a0"], lp["a2"], lp["g2"],
              lp["k_k"], lp["k_a"], lp["r_k"], lp["seg"]]
    args = [zr, zl, *consts]
    in_specs = [row(3 * d), row(LORA_PAD)] + [full(a) for a in consts]
    if has_vres:
        extra = [u, v_first, lp["v0"], lp["v1"], lp["v2"]]
        args += extra
        in_specs += [row(d), row(d)] + [full(a) for a in extra[2:]]
    out_dt = [BF16, F32, BF16, BF16, BF16, BF16, BF16, BF16] + ([] if has_vres else [BF16])
    return pl.pallas_call(
        functools.partial(_prep_kernel, has_vres, seq // tm),
        grid=(t // tm,),
        in_specs=in_specs,
        out_specs=[row(d) for _ in out_dt],
        out_shape=[jax.ShapeDtypeStruct((t, d), dt) for dt in out_dt],
        scratch_shapes=[pltpu.VMEM((8, 3 * d), F32), pltpu.VMEM((8, LORA_PAD), F32)],
        compiler_params=_cparams(("arbitrary",)),
        name="rwkv_prep",
    )(*args)


def _scan_kernel(r_ref, lw_ref, k_ref, v_ref, a_ref, b_ref, bonus_ref, g_ref, lng_ref, lnb_ref,
                 o_ref, h_ref):
    c = pl.program_id(1)

    @pl.when(c == 0)
    def _():
        h_ref[...] = jnp.zeros_like(h_ref)

    C = SCAN_CHUNK
    n2 = HEADS_PER_VREG * C
    ri = lax.broadcasted_iota(jnp.int32, (C, C), 0)
    ci = lax.broadcasted_iota(jnp.int32, (C, C), 1)
    tri = (ri >= ci).astype(BF16)
    lane = lax.broadcasted_iota(jnp.int32, (1, LANES), 1)
    head_masks = [((lane // HEAD_DIM) == hh).astype(F32) for hh in range(HEADS_PER_VREG)]
    r2 = lax.broadcasted_iota(jnp.int32, (n2, n2), 0)
    c2 = lax.broadcasted_iota(jnp.int32, (n2, n2), 1)
    same = (r2 // C) == (c2 // C)
    strict = same & ((r2 % C) > (c2 % C))
    incl = same & ((r2 % C) >= (c2 % C))
    eye = (r2 == c2).astype(F32)
    lr = lax.broadcasted_iota(jnp.int32, (LANES, LANES), 0)
    lc = lax.broadcasted_iota(jnp.int32, (LANES, LANES), 1)
    seg_mean = jnp.where((lr // HEAD_DIM) == (lc // HEAD_DIM), 1.0 / HEAD_DIM, 0.0).astype(BF16)

    def stack(x):
        return jnp.concatenate([x * m for m in head_masks], axis=0).astype(BF16)

    n_sub = r_ref.shape[0] // C
    pairs = range(N_PAIRS)
    sls = [slice(p * LANES, (p + 1) * LANES) for p in pairs]
    items = [(s, p) for s in range(n_sub) for p in pairs]
    cums = []
    for s in range(n_sub):
        lw_s = lw_ref[s * C:(s + 1) * C, :]
        lw_hi = lw_s.astype(BF16)
        rem = lw_s - lw_hi.astype(F32)
        lw_mid = rem.astype(BF16)
        lw_lo = (rem - lw_mid.astype(F32)).astype(BF16)
        cums.append(_dot(tri, lw_hi) + (_dot(tri, lw_mid) + _dot(tri, lw_lo)))
    lhs, rhs, bk, v2, dec = [], [], [], [], []
    for s, p in items:
        rows, sl = slice(s * C, (s + 1) * C), sls[p]
        lw = lw_ref[rows, sl]
        cum = cums[s][:, sl]
        tot = cum[C - 1:C, :]
        e_bwd = jnp.exp(-cum)
        e_rest = jnp.exp(tot - cum)
        r = r_ref[rows, sl].astype(F32)
        k = k_ref[rows, sl].astype(F32)
        a = a_ref[rows, sl].astype(F32)
        b = b_ref[rows, sl].astype(F32)
        lhs.append(jnp.concatenate([stack(a * jnp.exp(cum - lw)), stack(r * jnp.exp(cum))], axis=0))
        rhs.append(jnp.concatenate([stack(b * e_bwd), stack(k * e_bwd)], axis=0))
        bk.append(jnp.concatenate([stack(b * e_rest), stack(k * e_rest)], axis=0))
        v2.append(stack(v_ref[rows, sl].astype(F32)))
        dec.append(jnp.exp(tot))
    amat = [_dot_nt(l, rr) for l, rr in zip(lhs, rhs)]
    a_ab = [jnp.where(strict, m[:n2, :n2], 0.0) for m in amat]
    a_ak = [jnp.where(strict, m[:n2, n2:], 0.0).astype(BF16) for m in amat]
    a_r = [jnp.concatenate([jnp.where(incl, m[n2:, :n2], 0.0), jnp.where(incl, m[n2:, n2:], 0.0)],
                           axis=1).astype(BF16) for m in amat]
    tinv = [eye + m for m in a_ab]
    ab = [m.astype(BF16) for m in a_ab]
    pw = [_dot(m, m) for m in ab]
    levels = int(np.log2(C)) - 1
    for lvl in range(levels):
        pb = [m.astype(BF16) for m in pw]
        if lvl < levels - 1:
            res = [_dot(q, jnp.concatenate([t.astype(BF16), q], axis=1)) for q, t in zip(pb, tinv)]
            tinv = [t + m[:, :n2] for t, m in zip(tinv, res)]
            pw = [m[:, n2:] for m in res]
        else:
            tinv = [t + _dot(q, t.astype(BF16)) for q, t in zip(pb, tinv)]
    ht = [h_ref[p] for p in pairs]
    ys = []
    for s in range(n_sub):
        it = [s * N_PAIRS + p for p in pairs]
        ah = [_dot_nt(lhs[i], ht[p].astype(BF16)) for p, i in zip(pairs, it)]
        w = [ah[p][:n2] + _dot(a_ak[i], v2[i]) for p, i in zip(pairs, it)]
        u = [_dot(tinv[i].astype(BF16), w[p].astype(BF16)) for p, i in zip(pairs, it)]
        uv = [jnp.concatenate([u[p].astype(BF16), v2[i]], axis=0) for p, i in zip(pairs, it)]
        y2 = [ah[p][n2:] + _dot(a_r[i], uv[p]) for p, i in zip(pairs, it)]
        ht = [ht[p] * dec[i] + _dot_tn(uv[p], bk[i]) for p, i in zip(pairs, it)]
        for m in y2:
            y = m[:C]
            for hh in range(1, HEADS_PER_VREG):
                y = y + m[hh * C:(hh + 1) * C]
            ys.append(y)
    for p in pairs:
        h_ref[p] = ht[p]
    y = jnp.concatenate(ys, axis=0)
    y_hi = y.astype(BF16)
    ym = _dot(y_hi, seg_mean) + _dot((y - y_hi.astype(F32)).astype(BF16), seg_mean)
    yc = y - ym
    yv = _dot((yc * yc).astype(BF16), seg_mean)
    yn = yc * lax.rsqrt(yv + GN_EPS)
    for i, (s, p) in enumerate(items):
        rows, sl = slice(s * C, (s + 1) * C), sls[p]
        out = yn[i * C:(i + 1) * C] * lng_ref[:, sl] + lnb_ref[:, sl]
        out = (out + bonus_ref[rows, sl].astype(F32)) * g_ref[rows, sl].astype(F32)
        o_ref[rows, sl] = out.astype(o_ref.dtype)


def _scan_call(r, lw, k, v, a, b, bonus, g, lng, lnb, batch, seq):
    t, d = r.shape
    rows = SCAN_SUB * SCAN_CHUNK
    nc = seq // rows
    blk = pl.BlockSpec((rows, d), lambda bi, ci: (bi * nc + ci, 0))
    full = pl.BlockSpec((1, d), lambda bi, ci: (0, 0))
    return pl.pallas_call(
        _scan_kernel,
        grid=(batch, nc),
        in_specs=[blk] * 8 + [full, full],
        out_specs=blk,
        out_shape=jax.ShapeDtypeStruct((t, d), BF16),
        scratch_shapes=[pltpu.VMEM((N_PAIRS, LANES, LANES), F32)],
        compiler_params=_cparams(("parallel", "arbitrary")),
        name="wkv7_scan",
    )(r, lw, k, v, a, b, bonus, g, lng, lnb)


def _attn_kernel(span, q_ref, kp_ref, kc_ref, vp_ref, vc_ref, o_ref, lse_ref):
    n = pl.program_id(2)
    Q = ATTN_BLOCK
    qi = lax.broadcasted_iota(jnp.int32, (Q, 2 * Q), 0)
    kj = lax.broadcasted_iota(jnp.int32, (Q, 2 * Q), 1)
    dist = Q + qi - kj
    band = (dist >= 0) & (dist <= span)
    first = band & ((n > 0) | (kj >= Q))
    lane = lax.broadcasted_iota(jnp.int32, (1, LANES), 1)
    masks = [(lane // HEAD_DIM) == hh for hh in range(HEADS_PER_VREG)]
    sls = [slice(p * LANES, (p + 1) * LANES) for p in range(ATTN_GW // LANES)]
    qbs = range(ATTN_QB)

    def keys(prev_ref, cur_ref, j, sl):
        lo = prev_ref[:, sl] if j == 0 else cur_ref[(j - 1) * Q:j * Q, sl]
        return jnp.concatenate([lo, cur_ref[j * Q:(j + 1) * Q, sl]], axis=0)

    chains = [(j, sl, m) for j in qbs for sl in sls for m in masks]
    scores = []
    for j, sl, m in chains:
        q = q_ref[j * Q:(j + 1) * Q, sl]
        scores.append(_dot_nt(jnp.where(m, q, jnp.zeros_like(q)), keys(kp_ref, kc_ref, j, sl)))
    probs, lses = [], []
    for (j, _, _), s in zip(chains, scores):
        s = jnp.where(first if j == 0 else band, s, NEG_INF)
        mx = jnp.max(s, axis=-1, keepdims=True)
        e = jnp.exp(s - mx)
        den = jnp.sum(e, axis=-1, keepdims=True)
        probs.append((e / den).astype(BF16))
        lses.append(mx + jnp.log(den))
    nh = len(masks)
    for c0 in range(0, len(chains), nh):
        j, sl, _ = chains[c0]
        vv = keys(vp_ref, vc_ref, j, sl)
        out = None
        lse = None
        for hh, m in enumerate(masks):
            o = _dot(probs[c0 + hh], jnp.where(m, vv, jnp.zeros_like(vv)))
            l = lses[c0 + hh]
            out = o if out is None else out + o
            lse = jnp.broadcast_to(l, (Q, LANES)) if lse is None else jnp.where(m, l, lse)
        o_ref[j * Q:(j + 1) * Q, sl] = out.astype(o_ref.dtype)
        lse_ref[j * Q:(j + 1) * Q, sl] = lse


def _attn_call(q, k, v, window, dilation):
    batch, _, L, _ = q.shape
    Q = ATTN_BLOCK
    nb = L // (ATTN_QB * Q)
    cur = pl.BlockSpec((None, None, ATTN_QB * Q, ATTN_GW), lambda b, r, n: (b, r, n, 0))
    prev = pl.BlockSpec((None, None, Q, ATTN_GW),
                        lambda b, r, n: (b, r, jnp.maximum(ATTN_QB * n - 1, 0), 0))
    return pl.pallas_call(
        functools.partial(_attn_kernel, window // dilation),
        grid=(batch, dilation, nb),
        in_specs=[cur, prev, cur, prev, cur],
        out_specs=[cur, cur],
        out_shape=[jax.ShapeDtypeStruct(q.shape, BF16), jax.ShapeDtypeStruct(q.shape, F32)],
        compiler_params=_cparams(("parallel", "parallel", "arbitrary")),
        name=f"dilated_attn_d{dilation}",
    )(q, k, k, v, v)


def _route(logits, bias):
    mxl = jnp.max(logits, axis=0, keepdims=True)
    ex = jnp.exp(logits - mxl)
    probs = ex / jnp.sum(ex, axis=0, keepdims=True)
    sel = probs + bias
    rows = [sel[e:e + 1, :] for e in range(N_EXPERTS)]
    n_groups = N_EXPERTS // EXPERTS_PER_GROUP
    scores = []
    for gi in range(n_groups):
        a, b, c, dd = rows[EXPERTS_PER_GROUP * gi:EXPERTS_PER_GROUP * (gi + 1)]
        hi1, lo1 = jnp.maximum(a, b), jnp.minimum(a, b)
        hi2, lo2 = jnp.maximum(c, dd), jnp.minimum(c, dd)
        scores.append(jnp.maximum(hi1, hi2) + jnp.maximum(jnp.minimum(hi1, hi2), jnp.maximum(lo1, lo2)))
    best_score = scores[0]
    best = jnp.zeros_like(scores[0], dtype=jnp.int32)
    for gi in range(1, n_groups):
        better = scores[gi] > best_score
        best = jnp.where(better, gi, best)
        best_score = jnp.where(better, scores[gi], best_score)
    eid = lax.broadcasted_iota(jnp.int32, sel.shape, 0)
    masked = jnp.where((eid // EXPERTS_PER_GROUP) == best, sel, -jnp.inf)
    m1 = jnp.max(masked, axis=0, keepdims=True)
    i1 = jnp.min(jnp.where(masked == m1, eid, N_EXPERTS), axis=0, keepdims=True)
    masked2 = jnp.where(eid == i1, -jnp.inf, masked)
    m2 = jnp.max(masked2, axis=0, keepdims=True)
    i2 = jnp.min(jnp.where(masked2 == m2, eid, N_EXPERTS), axis=0, keepdims=True)
    p1 = jnp.sum(jnp.where(eid == i1, probs, 0.0), axis=0, keepdims=True)
    p2 = jnp.sum(jnp.where(eid == i2, probs, 0.0), axis=0, keepdims=True)
    tot = p1 + p2
    return jnp.where(eid == i1, p1 / tot, 0.0) + jnp.where(eid == i2, p2 / tot, 0.0)


def _out_kernel(conv_ref, yb_ref, o0_ref, o1_ref, o2_ref, l0_ref, l1_ref, l2_ref, zg_ref, h_ref,
                pa_ref, pb_ref, pc_ref, wo_ref, g_ref, b_ref, wr_ref, rb_ref, h1_ref, comb_ref, cnt_ref, sc_ref):
    d = D_MODEL
    tm = h_ref.shape[0]
    hm = tm // OUT_SPLIT
    nl = ATTN_GW // LANES
    dils = [dil for _, dil in ATTN_GROUPS]
    slot = iter(range(sc_ref.shape[0]))

    def token_order(ref, dil, s):
        n = hm // dil
        if dil == 1:
            return ref[0, s * hm:(s + 1) * hm, :].astype(F32)
        buf = next(slot)
        for res in range(dil):
            blk = ref[res, s * n:(s + 1) * n, :].astype(F32)
            for j in range(nl):
                sc_ref[buf, j, pl.ds(res, n, stride=dil), :] = blk[:, j * LANES:(j + 1) * LANES]
        return jnp.concatenate([sc_ref[buf, j] for j in range(nl)], axis=1)

    subs = range(OUT_SPLIT)
    rows = [slice(s * hm, (s + 1) * hm) for s in subs]
    ycs = []
    for s in subs:
        l0, l1, l2 = [token_order(r, dl, s) for r, dl in zip((l0_ref, l1_ref, l2_ref), dils)]
        mx = jnp.maximum(jnp.maximum(l0, l1), l2)
        e0, e1, e2 = jnp.exp(l0 - mx), jnp.exp(l1 - mx), jnp.exp(l2 - mx)
        yc = e0 * token_order(o0_ref, dils[0], s)
        yc = yc + e1 * token_order(o1_ref, dils[1], s)
        yc = (yc + e2 * token_order(o2_ref, dils[2], s)) / (e0 + e1 + e2)
        ycs.append(yc.astype(BF16))
    ya = [_dot(conv_ref[r, :], pa_ref[...]) for r in rows]
    yb = [_dot(yb_ref[r, :], pb_ref[...]) for r in rows]
    yc = [_dot(ycs[s], pc_ref[...]) for s in subs]
    mixes = []
    for s, r in enumerate(rows):
        ga = zg_ref[r, 0:d].astype(F32)
        gb = zg_ref[r, d:2 * d].astype(F32)
        gc = zg_ref[r, 2 * d:3 * d].astype(F32)
        mixes.append((ga * ya[s] + gb * yb[s] + gc * yc[s]).astype(BF16))
    outs = [_dot(m, wo_ref[...]) for m in mixes]
    h1s = []
    for s, r in enumerate(rows):
        h1 = _layer_norm(ALPHA * h_ref[r, :] + outs[s], g_ref[...], b_ref[...])
        h1_ref[r, :] = h1
        h1s.append(h1)
    logits = [lax.dot_general(wr_ref[...], h1, (((1,), (1,)), ((), ())), preferred_element_type=F32,
                              precision=lax.Precision.HIGHEST) for h1 in h1s]
    count = jnp.zeros((N_EXPERTS, 1), F32)
    for s, r in enumerate(rows):
        comb = _route(logits[s], rb_ref[...])
        comb_ref[:, r] = comb
        count = count + jnp.sum((comb > 0.0).astype(F32), axis=1, keepdims=True)
    cnt_ref[...] = jnp.broadcast_to(count, cnt_ref.shape)


def _out_call(conv, yb, outs, lses, z, h, lp, w_router_t, router_bias, seq, tm=512):
    t, d = h.shape
    tps = seq // tm
    row = lambda w, j=0: pl.BlockSpec((tm, w), lambda i, j=j: (i, j))
    full = lambda a: pl.BlockSpec(a.shape, lambda i: (0,) * a.ndim)
    res = [pl.BlockSpec((None, dil, tm // dil, ATTN_GW), lambda i: (i // tps, 0, i % tps, 0))
           for _, dil in ATTN_GROUPS]
    consts = [lp["p_a"], lp["p_b"], lp["p_c"], lp["w_o"], lp["ln1_g"], lp["ln1_b"], w_router_t, router_bias]
    n_relayout = 2 * OUT_SPLIT * sum(1 for _, dil in ATTN_GROUPS if dil > 1)
    return pl.pallas_call(
        _out_kernel,
        grid=(t // tm,),
        in_specs=[row(d), row(d)] + res + res + [row(3 * d), row(d)]
                 + [full(a) for a in consts],
        out_specs=[row(d), pl.BlockSpec((N_EXPERTS, tm), lambda i: (0, i)),
                   pl.BlockSpec((None, N_EXPERTS, LANES), lambda i: (i, 0, 0))],
        out_shape=[jax.ShapeDtypeStruct((t, d), F32), jax.ShapeDtypeStruct((N_EXPERTS, t), F32),
                   jax.ShapeDtypeStruct((t // tm, N_EXPERTS, LANES), F32)],
        scratch_shapes=[pltpu.VMEM((n_relayout, ATTN_GW // LANES, tm // OUT_SPLIT, LANES), F32)],
        compiler_params=_cparams(("parallel",)),
        name="mixer_out",
    )(conv, yb, *outs, *lses, z, h, *consts)


def _moe_kernel(cnt_ref, x_ref, comb_ref, wg_ref, wu_ref, wd_ref, g_ref, b_ref, o_ref,
                xb_ref, acc_ref, rank_ref):
    i = pl.program_id(0)
    e = pl.program_id(1)
    tm = x_ref.shape[0]
    R = MOE_ROWS

    @pl.when(e == 0)
    def _():
        xb_ref[...] = x_ref[...].astype(BF16)
        acc_ref[...] = jnp.zeros_like(acc_ref)
        sel = (comb_ref[...] > 0.0).astype(BF16)
        tri = (lax.broadcasted_iota(jnp.int32, (tm, tm), 0)
               <= lax.broadcasted_iota(jnp.int32, (tm, tm), 1)).astype(BF16)
        rank_ref[...] = _dot(sel, tri) - 1.0

    comb_e = comb_ref[pl.ds(e, 1), :]
    rank_e = jnp.where(comb_e > 0.0, rank_ref[pl.ds(e, 1), :], -1.0)
    slot = lax.broadcasted_iota(jnp.int32, (R, tm), 0).astype(F32)

    def block(blk, carry):
        hit = rank_e == slot + (blk * R).astype(F32)
        onehot = hit.astype(BF16)
        xg = _dot(onehot, xb_ref[...]).astype(BF16)
        cw = jnp.sum(jnp.where(hit, comb_e, 0.0), axis=1, keepdims=True)
        gate = _dot(xg, wg_ref[...])
        up = _dot(xg, wu_ref[...])
        hid = gate * _sigmoid(gate) * up * cw
        y = _dot(hid.astype(BF16), wd_ref[...]).astype(BF16)
        acc_ref[...] += _dot_tn(onehot, y)
        return carry

    lax.fori_loop(0, (cnt_ref[i, e] + R - 1) // R, block, 0)

    @pl.when(e == N_EXPERTS - 1)
    def _():
        o_ref[...] = _layer_norm(ALPHA * x_ref[...] + acc_ref[...], g_ref[...], b_ref[...])


def _moe_call(h1, comb_t, counts, lp, tm=1024):
    t, d = h1.shape
    wsp = lambda s: pl.BlockSpec((None,) + s, lambda i, e, c: (e, 0, 0))
    vec = pl.BlockSpec((1, d), lambda i, e, c: (0, 0))
    return pl.pallas_call(
        _moe_kernel,
        grid_spec=pltpu.PrefetchScalarGridSpec(
            num_scalar_prefetch=1,
            grid=(t // tm, N_EXPERTS),
            in_specs=[pl.BlockSpec((tm, d), lambda i, e, c: (i, 0)),
                      pl.BlockSpec((N_EXPERTS, tm), lambda i, e, c: (0, i)),
                      wsp((d, D_EXPERT)), wsp((d, D_EXPERT)), wsp((D_EXPERT, d)), vec, vec],
            out_specs=pl.BlockSpec((tm, d), lambda i, e, c: (i, 0)),
            scratch_shapes=[pltpu.VMEM((tm, d), BF16), pltpu.VMEM((tm, d), F32),
                            pltpu.VMEM((N_EXPERTS, tm), F32)]),
        out_shape=jax.ShapeDtypeStruct((t, d), F32),
        compiler_params=_cparams(("parallel", "arbitrary")),
        name="moe",
    )(counts, h1, comb_t, lp["w_gate"], lp["w_up"], lp["w_down"], lp["ln2_g"], lp["ln2_b"])


def _pad_rows(w, start, total):
    return jnp.zeros((total, w.shape[1]), w.dtype).at[start:start + w.shape[0]].set(w)


def _rotary_tables(positions):
    half = ROT_DIM // 2
    inv_freq = ROPE_THETA ** (-jnp.arange(0, ROT_DIM, 2, dtype=F32) / ROT_DIM)
    ang = positions.astype(F32).reshape(-1, 1) * inv_freq
    cos, sin = jnp.cos(ang), jnp.sin(ang)
    t = ang.shape[0]
    pad = HEAD_DIM - ROT_DIM
    rc = jnp.concatenate([cos, cos, jnp.ones((t, pad), F32)], axis=1)
    rm = jnp.concatenate([-sin, jnp.zeros((t, half + pad), F32)], axis=1)
    rp = jnp.concatenate([jnp.zeros((t, half), F32), sin, jnp.zeros((t, pad), F32)], axis=1)
    return tuple(jnp.tile(x, (1, HEADS_PER_VREG)) for x in (rc, rm, rp))


def _pack_in_proj(w):
    d = D_MODEL
    n_lora = DECAY_LORA + AAA_LORA + GATE_LORA
    gate, conv, rkv = w[:, :3 * d], w[:, 3 * d:6 * d], w[:, 6 * d:9 * d]
    lora = jnp.pad(w[:, 9 * d:9 * d + n_lora], ((0, 0), (0, LORA_PAD - n_lora)))
    attn = w[:, 9 * d + n_lora:]
    q, k, v = attn[:, :ATTN_W], attn[:, ATTN_W:2 * ATTN_W], attn[:, 2 * ATTN_W:]
    vg = [v[:, gi * ATTN_GW:(gi + 1) * ATTN_GW] for gi in range(len(ATTN_GROUPS))]
    cat = lambda parts: jnp.concatenate(parts, axis=1).astype(BF16)
    return {"w_pg": cat([gate, lora]), "w_pc": cat([conv, vg[0]]), "w_pr": cat([rkv, vg[1]]),
            "w_pa": cat([q, k, vg[2]])}


def kernel(x, positions, ln_in_g, ln_in_b, w_in, conv_w, rwkv_mu, rwkv_w0, rwkv_w2, rwkv_a0, rwkv_a2, rwkv_g2, rwkv_v0, rwkv_v1, rwkv_v2, rwkv_k_k, rwkv_k_a, rwkv_r_k, rwkv_lnx_g, rwkv_lnx_b, p_a, p_b, p_c, w_o, ln1_g, ln1_b, w_router, router_bias, w_gate, w_up, w_down, ln2_g, ln2_b):
    batch, seq, d = x.shape
    t = batch * seq
    depth = w_in.shape[0]
    assert d == D_MODEL and seq % (ATTN_QB * ATTN_BLOCK * ATTN_GROUPS[-1][1]) == 0
    rot_tabs = _rotary_tables(positions)
    hid = lax.broadcasted_iota(jnp.int32, (d, d), 0) // HEAD_DIM
    seg = (hid == hid.T).astype(BF16)
    w_router_t = w_router.T.astype(F32)
    rbias = router_bias.reshape(N_EXPERTS, 1).astype(F32)
    vec = lambda a: a.reshape(1, -1).astype(F32)

    h = _ln_call(x.reshape(t, d), ln_in_g, ln_in_b)
    v_first = None
    for l in range(depth):
        lp = {
            "conv_w": conv_w[l],
            "mu_r": vec(rwkv_mu[l][:3 * d]),
            "mu_l": vec(jnp.pad(rwkv_mu[l][3 * d:], (0, LORA_PAD - 288))),
            "w0": vec(rwkv_w0[l]),
            "w2": _pad_rows(rwkv_w2[l], 0, LORA_PAD).astype(BF16),
            "a0": vec(rwkv_a0[l]),
            "a2": _pad_rows(rwkv_a2[l], DECAY_LORA, LORA_PAD).astype(BF16),
            "g2": _pad_rows(rwkv_g2[l], DECAY_LORA + AAA_LORA, LORA_PAD).astype(BF16),
            "k_k": vec(rwkv_k_k[l]), "k_a": vec(rwkv_k_a[l]), "r_k": vec(rwkv_r_k[l]),
            "seg": seg,
            "p_a": p_a[l].astype(BF16), "p_b": p_b[l].astype(BF16), "p_c": p_c[l].astype(BF16),
            "w_o": w_o[l].astype(BF16), "ln1_g": vec(ln1_g[l]), "ln1_b": vec(ln1_b[l]),
            "w_gate": w_gate[l].astype(BF16), "w_up": w_up[l].astype(BF16), "w_down": w_down[l].astype(BF16),
            "ln2_g": vec(ln2_g[l]), "ln2_b": vec(ln2_b[l]),
        }
        if l > 0:
            lp["v0"] = vec(rwkv_v0[l - 1])
            lp["v1"] = jnp.pad(rwkv_v1[l - 1], ((0, 0), (0, MV_PAD - MV_LORA))).astype(BF16)
            lp["v2"] = _pad_rows(rwkv_v2[l - 1], 0, MV_PAD).astype(BF16)
        lp.update(_pack_in_proj(w_in[l]))
        gates, lora, conv, zr, q_g, k_g, v_g = _in_proj(h, lp, rot_tabs, batch, seq)
        prep = _prep_call(zr, lora, lp, h if l > 0 else None, v_first, seq)
        r, lw, k, v, a, b, g, bonus = prep[:8]
        if l == 0:
            v_first = prep[8]
        yb = _scan_call(r, lw, k, v, a, b, bonus, g, vec(rwkv_lnx_g[l]), vec(rwkv_lnx_b[l]), batch, seq)
        outs, lses = [], []
        for gi, (window, dilation) in enumerate(ATTN_GROUPS):
            o, s = _attn_call(q_g[gi], k_g[gi], v_g[gi], window, dilation)
            outs.append(o)
            lses.append(s)
        h1, comb_t, cnt = _out_call(conv, yb, outs, lses, gates, h, lp, w_router_t, rbias, seq)
        counts = cnt[:, :, 0].reshape(t // MOE_TM, -1, N_EXPERTS).sum(axis=1).astype(jnp.int32)
        h = _moe_call(h1, comb_t, counts, lp, MOE_TM)
    return h.reshape(batch, seq, d)
```

```python
import functools

import jax
import jax.numpy as jnp
import numpy as np
from jax import lax
from jax.experimental import pallas as pl
from jax.experimental.pallas import tpu as pltpu

F32 = jnp.float32
BF16 = jnp.bfloat16

D_MODEL = 1024
HEAD_DIM = 64
N_HEADS = D_MODEL // HEAD_DIM
LANES = 128
HEADS_PER_VREG = LANES // HEAD_DIM
N_PAIRS = D_MODEL // LANES
DECAY_LORA = 64
AAA_LORA = 64
GATE_LORA = 160
MV_LORA = 32
LORA_PAD = 384
MV_PAD = 128
GN_EPS = 1e-5 * HEAD_DIM
ATTN_GROUPS = ((128, 1), (512, 4), (2048, 16))
ATTN_HEADS = 8
ATTN_GW = ATTN_HEADS * HEAD_DIM
ATTN_W = len(ATTN_GROUPS) * ATTN_GW
ATTN_BLOCK = 128
ATTN_QB = 2
ROT_DIM = 16
ROPE_THETA = 500000.0
N_EXPERTS = 16
EXPERTS_PER_GROUP = 4
D_EXPERT = 512
DEPTH = 2
ALPHA = (2 * DEPTH) ** 0.25
LN_EPS = 1e-5
NEG_INF = -1e30
SCAN_CHUNK = 64
SCAN_SUB = 4
OUT_SPLIT = 2
MOE_TM = 1024
MOE_ROWS = 176

VMEM_LIMIT = 56 * 1024 * 1024


def _cparams(sem):
    return pltpu.CompilerParams(dimension_semantics=sem, vmem_limit_bytes=VMEM_LIMIT)


def _layer_norm(x, g, b):
    mu = jnp.mean(x, axis=-1, keepdims=True)
    xc = x - mu
    var = jnp.mean(xc * xc, axis=-1, keepdims=True)
    return xc * lax.rsqrt(var + LN_EPS) * g + b


def _sigmoid(x):
    return 1.0 / (1.0 + jnp.exp(-x))


def _dot(a, b):
    return jnp.dot(a, b, preferred_element_type=F32)


def _dot_nt(a, b):
    return lax.dot_general(a, b, (((1,), (1,)), ((), ())), preferred_element_type=F32)


def _dot_tn(a, b):
    return lax.dot_general(a, b, (((0,), (0,)), ((), ())), preferred_element_type=F32)


def _ln_kernel(x_ref, g_ref, b_ref, o_ref):
    o_ref[...] = _layer_norm(x_ref[...], g_ref[...], b_ref[...])


def _ln_call(x, g, b, tm=512):
    t, d = x.shape
    return pl.pallas_call(
        _ln_kernel,
        grid=(t // tm,),
        in_specs=[pl.BlockSpec((tm, d), lambda i: (i, 0)),
                  pl.BlockSpec((1, d), lambda i: (0, 0)),
                  pl.BlockSpec((1, d), lambda i: (0, 0))],
        out_specs=pl.BlockSpec((tm, d), lambda i: (i, 0)),
        out_shape=jax.ShapeDtypeStruct((t, d), F32),
        compiler_params=_cparams(("parallel",)),
        name="ln_in",
    )(x, g.reshape(1, d), b.reshape(1, d))


def _shift_rows(x, carry8, n):
    rolled = pltpu.roll(x, n, axis=0)
    cr = pltpu.roll(carry8, n, axis=0)
    row = lax.broadcasted_iota(jnp.int32, (8, x.shape[1]), 0)
    top = jnp.where(row < n, cr, rolled[:8])
    return jnp.concatenate([top, rolled[8:]], axis=0)


def _store_residue_major(dst, tg, dil, c_rows):
    tm = tg.shape[0]
    if dil == 1:
        dst[0] = tg.astype(dst.dtype)
        return
    nl = ATTN_GW // LANES
    for j in range(nl):
        c_rows[j] = tg[:, j * LANES:(j + 1) * LANES]
    for res in range(dil):
        for j in range(nl):
            dst[res, :, j * LANES:(j + 1) * LANES] = (
                c_rows[j, pl.ds(res, tm // dil, stride=dil), :].astype(dst.dtype))


def _proj_gate_kernel(x_ref, w_ref, gate_o, lora_o):
    acc = _dot(x_ref[...].astype(BF16), w_ref[...])
    d3 = 3 * D_MODEL
    gate_o[...] = _sigmoid(acc[:, :d3]).astype(gate_o.dtype)
    lora_o[...] = acc[:, d3:].astype(lora_o.dtype)


def _proj_conv_kernel(tiles_per_seq, x_ref, w_ref, convw_ref, conv_o, v0_o, carry):
    i = pl.program_id(0)

    @pl.when(i % tiles_per_seq == 0)
    def _():
        carry[...] = jnp.zeros_like(carry)

    acc = _dot(x_ref[...].astype(BF16), w_ref[...])
    tm = acc.shape[0]
    d = D_MODEL
    ch = acc[:, d:2 * d] * acc[:, 2 * d:3 * d]
    cr = carry[...]
    conv = (convw_ref[0:1, :] * _shift_rows(ch, cr, 2)
            + convw_ref[1:2, :] * _shift_rows(ch, cr, 1)
            + convw_ref[2:3, :] * ch)
    conv_o[...] = (acc[:, :d] * conv).astype(conv_o.dtype)
    carry[...] = ch[tm - 8:, :]
    v0_o[0] = acc[:, 3 * d:].astype(v0_o.dtype)


def _proj_rkv_kernel(dil, x_ref, w_ref, zr_o, v_o, c_rows):
    acc = _dot(x_ref[...].astype(BF16), w_ref[...])
    d3 = 3 * D_MODEL
    zr_o[...] = acc[:, :d3].astype(zr_o.dtype)
    _store_residue_major(v_o, acc[:, d3:], dil, c_rows)


def _proj_attn_kernel(dil_v, x_ref, w_ref, rc_ref, rm_ref, rp_ref, *rest):
    n_g = len(ATTN_GROUPS)
    q_o, k_o, v_o, c_rows = rest[:n_g], rest[n_g:2 * n_g], rest[2 * n_g], rest[2 * n_g + 1]
    acc = _dot(x_ref[...].astype(BF16), w_ref[...])
    reps = ATTN_W // LANES
    rc = jnp.concatenate([rc_ref[...]] * reps, axis=1)
    rm = jnp.concatenate([rm_ref[...]] * reps, axis=1)
    rp = jnp.concatenate([rp_ref[...]] * reps, axis=1)
    half = ROT_DIM // 2
    for which, (dsts, scale) in enumerate(((q_o, HEAD_DIM ** -0.5), (k_o, 1.0))):
        t = acc[:, which * ATTN_W:(which + 1) * ATTN_W]
        t = (t * rc + pltpu.roll(t, ATTN_W - half, axis=1) * rm + pltpu.roll(t, half, axis=1) * rp) * scale
        for gi, (_, dil) in enumerate(ATTN_GROUPS):
            _store_residue_major(dsts[gi], t[:, gi * ATTN_GW:(gi + 1) * ATTN_GW], dil, c_rows)
    _store_residue_major(v_o, acc[:, 2 * ATTN_W:], dil_v, c_rows)


def _proj_call(kernel, name, h, w, extra, extra_specs, out_specs, out_shape, scratch, sem, tm):
    t, d = h.shape
    return pl.pallas_call(
        kernel,
        grid=(t // tm,),
        in_specs=[pl.BlockSpec((tm, d), lambda i: (i, 0)), pl.BlockSpec(w.shape, lambda i: (0, 0))] + extra_specs,
        out_specs=out_specs,
        out_shape=out_shape,
        scratch_shapes=scratch,
        compiler_params=_cparams((sem,)),
        name=name,
    )(h, w, *extra)


def _in_proj(h, lp, rot_tabs, batch, seq, tm=512):
    t, d = h.shape
    tps = seq // tm
    row = lambda w: pl.BlockSpec((tm, w), lambda i: (i, 0))
    res_spec = lambda dil: pl.BlockSpec((None, dil, tm // dil, ATTN_GW), lambda i: (i // tps, 0, i % tps, 0))
    res_shape = lambda dil: jax.ShapeDtypeStruct((batch, dil, seq // dil, ATTN_GW), BF16)
    rows_scratch = pltpu.VMEM((ATTN_GW // LANES, tm, LANES), F32)
    dils = [dil for _, dil in ATTN_GROUPS]
    gates, lora = _proj_call(
        _proj_gate_kernel, "proj_gate", h, lp["w_pg"], [], [],
        [row(3 * d), row(LORA_PAD)],
        [jax.ShapeDtypeStruct((t, 3 * d), BF16), jax.ShapeDtypeStruct((t, LORA_PAD), BF16)],
        [], "parallel", tm)
    conv, v0 = _proj_call(
        functools.partial(_proj_conv_kernel, tps), "proj_conv", h, lp["w_pc"], [lp["conv_w"]],
        [pl.BlockSpec(lp["conv_w"].shape, lambda i: (0, 0))],
        [row(d), res_spec(dils[0])],
        [jax.ShapeDtypeStruct((t, d), BF16), res_shape(dils[0])],
        [pltpu.VMEM((8, d), F32)], "arbitrary", tm)
    zr, v1 = _proj_call(
        functools.partial(_proj_rkv_kernel, dils[1]), "proj_rkv", h, lp["w_pr"], [], [],
        [row(3 * d), res_spec(dils[1])],
        [jax.ShapeDtypeStruct((t, 3 * d), BF16), res_shape(dils[1])],
        [rows_scratch], "parallel", tm)
    qkv = _proj_call(
        functools.partial(_proj_attn_kernel, dils[2]), "proj_attn", h, lp["w_pa"], list(rot_tabs),
        [row(LANES)] * 3,
        [res_spec(dl) for dl in dils] * 2 + [res_spec(dils[2])],
        [res_shape(dl) for dl in dils] * 2 + [res_shape(dils[2])],
        [rows_scratch], "parallel", tm)
    n_g = len(ATTN_GROUPS)
    q, k, v2 = qkv[:n_g], qkv[n_g:2 * n_g], qkv[2 * n_g]
    return gates, lora, conv, zr, q, k, [v0, v1, v2]


def _prep_kernel(has_vres, tiles_per_seq,
                 zr_ref, zl_ref, mu_r_ref, mu_l_ref, w0_ref, w2_ref, a0_ref, a2_ref, g2_ref,
                 kk_ref, ka_ref, rk_ref, seg_ref, *rest):
    if has_vres:
        u_ref, vf_ref, v0_ref, v1_ref, v2_ref = rest[:5]
        rest = rest[5:]
    r_o, lw_o, k_o, v_o, a_o, b_o, g_o, bonus_o = rest[:8]
    rest = rest[8:]
    if not has_vres:
        vf_o = rest[0]
        rest = rest[1:]
    c_zr, c_zl = rest
    i = pl.program_id(0)

    @pl.when(i % tiles_per_seq == 0)
    def _():
        c_zr[...] = jnp.zeros_like(c_zr)
        c_zl[...] = jnp.zeros_like(c_zl)

    tm = zr_ref.shape[0]
    d = D_MODEL
    zr = zr_ref[...].astype(F32)
    zl = zl_ref[...].astype(F32)
    zr_m = zr + (_shift_rows(zr, c_zr[...], 1) - zr) * mu_r_ref[...]
    zl_m = zl + (_shift_rows(zl, c_zl[...], 1) - zl) * mu_l_ref[...]
    c_zr[...] = zr[tm - 8:, :]
    c_zl[...] = zl[tm - 8:, :]
    r = zr_m[:, :d]
    k = zr_m[:, d:2 * d]
    v = zr_m[:, 2 * d:]
    x = w0_ref[...] + _dot(jnp.tanh(zl_m).astype(BF16), w2_ref[...])
    lw_o[...] = -float(np.exp(-0.5)) * _sigmoid(x)
    if has_vres:
        lo = _dot(u_ref[...].astype(BF16), v1_ref[...])
        gate_v = _sigmoid(v0_ref[...] + _dot(lo.astype(BF16), v2_ref[...]))
        v = v + (vf_ref[...].astype(F32) - v) * gate_v
    else:
        vf_o[...] = v.astype(vf_o.dtype)
    zl_b = zl_m.astype(BF16)
    a_gate = _sigmoid(a0_ref[...] + _dot(zl_b, a2_ref[...]))
    g_o[...] = _dot(_sigmoid(zl_m).astype(BF16), g2_ref[...]).astype(g_o.dtype)
    seg = seg_ref[...]

    def head_sums(x):
        xb = x.astype(BF16)
        return jnp.concatenate([_dot(xb[:, p * LANES:(p + 1) * LANES], seg) for p in range(N_PAIRS)], axis=1)

    kk = k * kk_ref[...]
    kk = kk * lax.rsqrt(jnp.maximum(head_sums(kk * kk), 1e-24))
    k2 = k * (1.0 + (a_gate - 1.0) * ka_ref[...])
    bonus = head_sums(r * k2 * rk_ref[...]) * v
    r_o[...] = r.astype(r_o.dtype)
    k_o[...] = k2.astype(k_o.dtype)
    v_o[...] = v.astype(v_o.dtype)
    a_o[...] = (-kk).astype(a_o.dtype)
    b_o[...] = (kk * a_gate).astype(b_o.dtype)
    bonus_o[...] = bonus.astype(bonus_o.dtype)


def _prep_call(zr, zl, lp, u, v_first, seq, tm=512):
    t = zr.shape[0]
    d = D_MODEL
    has_vres = v_first is not None
    row = lambda w: pl.BlockSpec((tm, w), lambda i: (i, 0))
    full = lambda a: pl.BlockSpec(a.shape, lambda i: (0,) * a.ndim)
    consts = [lp["mu_r"], lp["mu_l"], lp["w0"], lp["w2"], lp["a0"], lp["a2"], lp["g2"],
              lp["k_k"], lp["k_a"], lp["r_k"], lp["seg"]]
    args = [zr, zl, *consts]
    in_specs = [row(3 * d), row(LORA_PAD)] + [full(a) for a in consts]
    if has_vres:
        extra = [u, v_first, lp["v0"], lp["v1"], lp["v2"]]
        args += extra
        in_specs += [row(d), row(d)] + [full(a) for a in extra[2:]]
    out_dt = [BF16, F32, BF16, BF16, BF16, BF16, BF16, BF16] + ([] if has_vres else [BF16])
    return pl.pallas_call(
        functools.partial(_prep_kernel, has_vres, seq // tm),
        grid=(t // tm,),
        in_specs=in_specs,
        out_specs=[row(d) for _ in out_dt],
        out_shape=[jax.ShapeDtypeStruct((t, d), dt) for dt in out_dt],
        scratch_shapes=[pltpu.VMEM((8, 3 * d), F32), pltpu.VMEM((8, LORA_PAD), F32)],
        compiler_params=_cparams(("arbitrary",)),
        name="rwkv_prep",
    )(*args)


def _scan_kernel(r_ref, lw_ref, k_ref, v_ref, a_ref, b_ref, bonus_ref, g_ref, lng_ref, lnb_ref,
                 o_ref, h_ref):
    c = pl.program_id(1)

    @pl.when(c == 0)
    def _():
        h_ref[...] = jnp.zeros_like(h_ref)

    C = SCAN_CHUNK
    n2 = HEADS_PER_VREG * C
    ri = lax.broadcasted_iota(jnp.int32, (C, C), 0)
    ci = lax.broadcasted_iota(jnp.int32, (C, C), 1)
    tri = (ri >= ci).astype(BF16)
    lane = lax.broadcasted_iota(jnp.int32, (1, LANES), 1)
    head_masks = [((lane // HEAD_DIM) == hh).astype(F32) for hh in range(HEADS_PER_VREG)]
    r2 = lax.broadcasted_iota(jnp.int32, (n2, 2 * n2), 0)
    c2 = lax.broadcasted_iota(jnp.int32, (n2, 2 * n2), 1)
    same = (r2 // C) == ((c2 % n2) // C)
    strict = same & ((r2 % C) > (c2 % C))
    incl = same & ((r2 % C) >= (c2 % C))
    strict_k = strict & (c2 >= n2)
    wn = max(n2, LANES)
    rw = lax.broadcasted_iota(jnp.int32, (n2, wn), 0)
    cw = lax.broadcasted_iota(jnp.int32, (n2, wn), 1)
    strict_b = ((rw // C) == (cw // C)) & ((rw % C) > (cw % C)) & (cw < n2)
    eye = (rw == cw).astype(F32)
    zero_rows = jnp.zeros((n2, LANES), BF16)

    def pad_rows(x):
        return x if wn == n2 else jnp.concatenate([x, jnp.zeros((wn - n2, x.shape[1]), x.dtype)], axis=0)
    lr = lax.broadcasted_iota(jnp.int32, (LANES, LANES), 0)
    lc = lax.broadcasted_iota(jnp.int32, (LANES, LANES), 1)
    seg_mean = jnp.where((lr // HEAD_DIM) == (lc // HEAD_DIM), 1.0 / HEAD_DIM, 0.0).astype(BF16)

    def stack(x):
        return jnp.concatenate([x * m for m in head_masks], axis=0).astype(BF16)

    n_sub = r_ref.shape[0] // C
    pairs = range(N_PAIRS)
    sls = [slice(p * LANES, (p + 1) * LANES) for p in pairs]
    items = [(s, p) for s in range(n_sub) for p in pairs]
    cums = []
    for s in range(n_sub):
        lw_s = lw_ref[s * C:(s + 1) * C, :]
        lw_hi = lw_s.astype(BF16)
        rem = lw_s - lw_hi.astype(F32)
        lw_mid = rem.astype(BF16)
        lw_lo = (rem - lw_mid.astype(F32)).astype(BF16)
        cums.append(_dot(tri, lw_hi) + (_dot(tri, lw_mid) + _dot(tri, lw_lo)))
    lhs, rhs, bk, v2, dec = [], [], [], [], []
    for s, p in items:
        rows, sl = slice(s * C, (s + 1) * C), sls[p]
        lw = lw_ref[rows, sl]
        cum = cums[s][:, sl]
        tot = cum[C - 1:C, :]
        e_bwd = jnp.exp(-cum)
        e_rest = jnp.exp(tot - cum)
        r = r_ref[rows, sl].astype(F32)
        k = k_ref[rows, sl].astype(F32)
        a = a_ref[rows, sl].astype(F32)
        b = b_ref[rows, sl].astype(F32)
        lhs.append(jnp.concatenate([stack(a * jnp.exp(cum - lw)), stack(r * jnp.exp(cum))], axis=0))
        rhs.append(jnp.concatenate([stack(b * e_bwd), stack(k * e_bwd)], axis=0))
        bk.append(jnp.concatenate([stack(b * e_rest), stack(k * e_rest)], axis=0))
        v2.append(stack(v_ref[rows, sl].astype(F32)))
        dec.append(jnp.exp(tot))
    amat = [_dot_nt(l, rr) for l, rr in zip(lhs, rhs)]
    a_ab = [jnp.where(strict_b, m[:n2, :wn], 0.0) for m in amat]
    a_ak = [jnp.where(strict_k, m[:n2, :], 0.0).astype(BF16) for m in amat]
    a_r = [jnp.where(incl, m[n2:, :], 0.0).astype(BF16) for m in amat]
    tinv = [eye + m for m in a_ab]
    pb = [m.astype(BF16) for m in a_ab]
    for _ in range(int(np.log2(C)) - 1):
        pb = [_dot(q, pad_rows(q)).astype(BF16) for q in pb]
        tinv = [t + _dot(q, pad_rows(t.astype(BF16))) for q, t in zip(pb, tinv)]
    ht = [h_ref[p] for p in pairs]
    ys = []
    for s in range(n_sub):
        it = [s * N_PAIRS + p for p in pairs]
        ah = [_dot_nt(lhs[i], ht[p].astype(BF16)) for p, i in zip(pairs, it)]
        w = [ah[p][:n2] + _dot(a_ak[i], jnp.concatenate([zero_rows, v2[i]], axis=0)) for p, i in zip(pairs, it)]
        u = [_dot(tinv[i].astype(BF16), pad_rows(w[p].astype(BF16))) for p, i in zip(pairs, it)]
        uv = [jnp.concatenate([u[p].astype(BF16), v2[i]], axis=0) for p, i in zip(pairs, it)]
        y2 = [ah[p][n2:] + _dot(a_r[i], uv[p]) for p, i in zip(pairs, it)]
        ht = [ht[p] * dec[i] + _dot_tn(uv[p], bk[i]) for p, i in zip(pairs, it)]
        for m in y2:
            y = m[:C]
            for hh in range(1, HEADS_PER_VREG):
                y = y + m[hh * C:(hh + 1) * C]
            ys.append(y)
    for p in pairs:
        h_ref[p] = ht[p]
    y = jnp.concatenate(ys, axis=0)
    y_hi = y.astype(BF16)
    ym = _dot(y_hi, seg_mean) + _dot((y - y_hi.astype(F32)).astype(BF16), seg_mean)
    yc = y - ym
    yv = _dot((yc * yc).astype(BF16), seg_mean)
    yn = yc * lax.rsqrt(yv + GN_EPS)
    for i, (s, p) in enumerate(items):
        rows, sl = slice(s * C, (s + 1) * C), sls[p]
        out = yn[i * C:(i + 1) * C] * lng_ref[:, sl] + lnb_ref[:, sl]
        out = (out + bonus_ref[rows, sl].astype(F32)) * g_ref[rows, sl].astype(F32)
        o_ref[rows, sl] = out.astype(o_ref.dtype)


def _scan_call(r, lw, k, v, a, b, bonus, g, lng, lnb, batch, seq):
    t, d = r.shape
    rows = SCAN_SUB * SCAN_CHUNK
    nc = seq // rows
    blk = pl.BlockSpec((rows, d), lambda bi, ci: (bi * nc + ci, 0))
    full = pl.BlockSpec((1, d), lambda bi, ci: (0, 0))
    return pl.pallas_call(
        _scan_kernel,
        grid=(batch, nc),
        in_specs=[blk] * 8 + [full, full],
        out_specs=blk,
        out_shape=jax.ShapeDtypeStruct((t, d), BF16),
        scratch_shapes=[pltpu.VMEM((N_PAIRS, LANES, LANES), F32)],
        compiler_params=_cparams(("parallel", "arbitrary")),
        name="wkv7_scan",
    )(r, lw, k, v, a, b, bonus, g, lng, lnb)


def _attn_kernel(span, q_ref, kp_ref, kc_ref, vp_ref, vc_ref, o_ref, lse_ref):
    n = pl.program_id(2)
    Q = ATTN_BLOCK
    qi = lax.broadcasted_iota(jnp.int32, (Q, 2 * Q), 0)
    kj = lax.broadcasted_iota(jnp.int32, (Q, 2 * Q), 1)
    dist = Q + qi - kj
    band = (dist >= 0) & (dist <= span)
    first = band & ((n > 0) | (kj >= Q))
    lane = lax.broadcasted_iota(jnp.int32, (1, LANES), 1)
    masks = [(lane // HEAD_DIM) == hh for hh in range(HEADS_PER_VREG)]
    sls = [slice(p * LANES, (p + 1) * LANES) for p in range(ATTN_GW // LANES)]
    qbs = range(ATTN_QB)

    def keys(prev_ref, cur_ref, j, sl):
        lo = prev_ref[:, sl] if j == 0 else cur_ref[(j - 1) * Q:j * Q, sl]
        return jnp.concatenate([lo, cur_ref[j * Q:(j + 1) * Q, sl]], axis=0)

    chains = [(j, sl, m) for j in qbs for sl in sls for m in masks]
    scores = []
    for j, sl, m in chains:
        q = q_ref[j * Q:(j + 1) * Q, sl]
        scores.append(_dot_nt(jnp.where(m, q, jnp.zeros_like(q)), keys(kp_ref, kc_ref, j, sl)))
    probs, lses = [], []
    for (j, _, _), s in zip(chains, scores):
        s = jnp.where(first if j == 0 else band, s, NEG_INF)
        mx = jnp.max(s, axis=-1, keepdims=True)
        e = jnp.exp(s - mx)
        den = jnp.sum(e, axis=-1, keepdims=True)
        probs.append((e / den).astype(BF16))
        lses.append(mx + jnp.log(den))
    nh = len(masks)
    for c0 in range(0, len(chains), nh):
        j, sl, _ = chains[c0]
        vv = keys(vp_ref, vc_ref, j, sl)
        out = None
        lse = None
        for hh, m in enumerate(masks):
            o = _dot(probs[c0 + hh], jnp.where(m, vv, jnp.zeros_like(vv)))
            l = lses[c0 + hh]
            out = o if out is None else out + o
            lse = jnp.broadcast_to(l, (Q, LANES)) if lse is None else jnp.where(m, l, lse)
        o_ref[j * Q:(j + 1) * Q, sl] = out.astype(o_ref.dtype)
        lse_ref[j * Q:(j + 1) * Q, sl] = lse


def _attn_call(q, k, v, window, dilation):
    batch, _, L, _ = q.shape
    Q = ATTN_BLOCK
    nb = L // (ATTN_QB * Q)
    cur = pl.BlockSpec((None, None, ATTN_QB * Q, ATTN_GW), lambda b, r, n: (b, r, n, 0))
    prev = pl.BlockSpec((None, None, Q, ATTN_GW),
                        lambda b, r, n: (b, r, jnp.maximum(ATTN_QB * n - 1, 0), 0))
    return pl.pallas_call(
        functools.partial(_attn_kernel, window // dilation),
        grid=(batch, dilation, nb),
        in_specs=[cur, prev, cur, prev, cur],
        out_specs=[cur, cur],
        out_shape=[jax.ShapeDtypeStruct(q.shape, BF16), jax.ShapeDtypeStruct(q.shape, F32)],
        compiler_params=_cparams(("parallel", "parallel", "arbitrary")),
        name=f"dilated_attn_d{dilation}",
    )(q, k, k, v, v)


def _route(logits, bias):
    mxl = jnp.max(logits, axis=0, keepdims=True)
    ex = jnp.exp(logits - mxl)
    probs = ex / jnp.sum(ex, axis=0, keepdims=True)
    sel = probs + bias
    rows = [sel[e:e + 1, :] for e in range(N_EXPERTS)]
    n_groups = N_EXPERTS // EXPERTS_PER_GROUP
    scores = []
    for gi in range(n_groups):
        a, b, c, dd = rows[EXPERTS_PER_GROUP * gi:EXPERTS_PER_GROUP * (gi + 1)]
        hi1, lo1 = jnp.maximum(a, b), jnp.minimum(a, b)
        hi2, lo2 = jnp.maximum(c, dd), jnp.minimum(c, dd)
        scores.append(jnp.maximum(hi1, hi2) + jnp.maximum(jnp.minimum(hi1, hi2), jnp.maximum(lo1, lo2)))
    best_score = scores[0]
    best = jnp.zeros_like(scores[0], dtype=jnp.int32)
    for gi in range(1, n_groups):
        better = scores[gi] > best_score
        best = jnp.where(better, gi, best)
        best_score = jnp.where(better, scores[gi], best_score)
    eid = lax.broadcasted_iota(jnp.int32, sel.shape, 0)
    masked = jnp.where((eid // EXPERTS_PER_GROUP) == best, sel, -jnp.inf)
    m1 = jnp.max(masked, axis=0, keepdims=True)
    i1 = jnp.min(jnp.where(masked == m1, eid, N_EXPERTS), axis=0, keepdims=True)
    masked2 = jnp.where(eid == i1, -jnp.inf, masked)
    m2 = jnp.max(masked2, axis=0, keepdims=True)
    i2 = jnp.min(jnp.where(masked2 == m2, eid, N_EXPERTS), axis=0, keepdims=True)
    p1 = jnp.sum(jnp.where(eid == i1, probs, 0.0), axis=0, keepdims=True)
    p2 = jnp.sum(jnp.where(eid == i2, probs, 0.0), axis=0, keepdims=True)
    tot = p1 + p2
    return jnp.where(eid == i1, p1 / tot, 0.0) + jnp.where(eid == i2, p2 / tot, 0.0)


def _out_kernel(conv_ref, yb_ref, o0_ref, o1_ref, o2_ref, l0_ref, l1_ref, l2_ref, zg_ref, h_ref,
                pa_ref, pb_ref, pc_ref, wo_ref, g_ref, b_ref, wr_ref, rb_ref, h1_ref, comb_ref, cnt_ref, sc_ref):
    d = D_MODEL
    tm = h_ref.shape[0]
    hm = tm // OUT_SPLIT
    nl = ATTN_GW // LANES
    dils = [dil for _, dil in ATTN_GROUPS]
    slot = iter(range(sc_ref.shape[0]))

    def token_order(ref, dil, s):
        n = hm // dil
        if dil == 1:
            return ref[0, s * hm:(s + 1) * hm, :].astype(F32)
        buf = next(slot)
        for res in range(dil):
            blk = ref[res, s * n:(s + 1) * n, :].astype(F32)
            for j in range(nl):
                sc_ref[buf, j, pl.ds(res, n, stride=dil), :] = blk[:, j * LANES:(j + 1) * LANES]
        return jnp.concatenate([sc_ref[buf, j] for j in range(nl)], axis=1)

    subs = range(OUT_SPLIT)
    rows = [slice(s * hm, (s + 1) * hm) for s in subs]
    ya, yb, ycs = [], [], []
    for s, r in enumerate(rows):
        ya.append(_dot(conv_ref[r, :], pa_ref[...]))
        l0, l1, l2 = [token_order(rf, dl, s) for rf, dl in zip((l0_ref, l1_ref, l2_ref), dils)]
        mx = jnp.maximum(jnp.maximum(l0, l1), l2)
        e0, e1, e2 = jnp.exp(l0 - mx), jnp.exp(l1 - mx), jnp.exp(l2 - mx)
        yb.append(_dot(yb_ref[r, :], pb_ref[...]))
        yc = e0 * token_order(o0_ref, dils[0], s)
        yc = yc + e1 * token_order(o1_ref, dils[1], s)
        yc = (yc + e2 * token_order(o2_ref, dils[2], s)) / (e0 + e1 + e2)
        ycs.append(yc.astype(BF16))
    yc_full = _dot(jnp.concatenate(ycs, axis=0), pc_ref[...])
    yc = [yc_full[r, :] for r in rows]
    mixes = []
    for s, r in enumerate(rows):
        ga = zg_ref[r, 0:d].astype(F32)
        gb = zg_ref[r, d:2 * d].astype(F32)
        gc = zg_ref[r, 2 * d:3 * d].astype(F32)
        mixes.append((ga * ya[s] + gb * yb[s] + gc * yc[s]).astype(BF16))
    outs = [_dot(m, wo_ref[...]) for m in mixes]
    wr = wr_ref[...]
    wr_hi = wr.astype(BF16)
    wr_hl = jnp.concatenate([wr_hi, (wr - wr_hi.astype(F32)).astype(BF16)], axis=0)
    logits = []
    for s, r in enumerate(rows):
        h1 = _layer_norm(ALPHA * h_ref[r, :] + outs[s], g_ref[...], b_ref[...])
        h1_ref[r, :] = h1
        h_hi = h1.astype(BF16)
        t1 = _dot_nt(wr_hl, h_hi)
        t2 = _dot_nt(wr_hi, (h1 - h_hi.astype(F32)).astype(BF16))
        logits.append(t1[:N_EXPERTS] + (t1[N_EXPERTS:] + t2))
    logits = jnp.concatenate(logits, axis=1)
    comb = _route(logits, rb_ref[...])
    comb_ref[...] = comb
    count = jnp.sum((comb > 0.0).astype(F32), axis=1, keepdims=True)
    cnt_ref[...] = jnp.broadcast_to(count, cnt_ref.shape)


def _out_call(conv, yb, outs, lses, z, h, lp, w_router_t, router_bias, seq, tm=512):
    t, d = h.shape
    tps = seq // tm
    row = lambda w, j=0: pl.BlockSpec((tm, w), lambda i, j=j: (i, j))
    full = lambda a: pl.BlockSpec(a.shape, lambda i: (0,) * a.ndim)
    res = [pl.BlockSpec((None, dil, tm // dil, ATTN_GW), lambda i: (i // tps, 0, i % tps, 0))
           for _, dil in ATTN_GROUPS]
    consts = [lp["p_a"], lp["p_b"], lp["p_c"], lp["w_o"], lp["ln1_g"], lp["ln1_b"], w_router_t, router_bias]
    n_relayout = 2 * OUT_SPLIT * sum(1 for _, dil in ATTN_GROUPS if dil > 1)
    return pl.pallas_call(
        _out_kernel,
        grid=(t // tm,),
        in_specs=[row(d), row(d)] + res + res + [row(3 * d), row(d)]
                 + [full(a) for a in consts],
        out_specs=[row(d), pl.BlockSpec((N_EXPERTS, tm), lambda i: (0, i)),
                   pl.BlockSpec((None, N_EXPERTS, LANES), lambda i: (i, 0, 0))],
        out_shape=[jax.ShapeDtypeStruct((t, d), F32), jax.ShapeDtypeStruct((N_EXPERTS, t), F32),
                   jax.ShapeDtypeStruct((t // tm, N_EXPERTS, LANES), F32)],
        scratch_shapes=[pltpu.VMEM((n_relayout, ATTN_GW // LANES, tm // OUT_SPLIT, LANES), F32)],
        compiler_params=_cparams(("parallel",)),
        name="mixer_out",
    )(conv, yb, *outs, *lses, z, h, *consts)


def _moe_kernel(cnt_ref, x_ref, comb_ref, wg_ref, wu_ref, wd_ref, g_ref, b_ref, o_ref,
                xb_ref, acc_ref, rank_ref):
    i = pl.program_id(0)
    e = pl.program_id(1)
    tm = x_ref.shape[0]
    R = MOE_ROWS

    @pl.when(e == 0)
    def _():
        xb_ref[...] = x_ref[...].astype(BF16)
        acc_ref[...] = jnp.zeros_like(acc_ref)
        sel = (comb_ref[...] > 0.0).astype(BF16)
        tri = (lax.broadcasted_iota(jnp.int32, (tm, tm), 0)
               <= lax.broadcasted_iota(jnp.int32, (tm, tm), 1)).astype(BF16)
        rank_ref[...] = _dot(sel, tri) - 1.0

    comb_e = comb_ref[pl.ds(e, 1), :]
    rank_e = jnp.where(comb_e > 0.0, rank_ref[pl.ds(e, 1), :], -1.0)
    slot = lax.broadcasted_iota(jnp.int32, (R, tm), 0).astype(F32)

    def block(blk, carry):
        hit = rank_e == slot + (blk * R).astype(F32)
        onehot = hit.astype(BF16)
        xg = _dot(onehot, xb_ref[...]).astype(BF16)
        cw = jnp.sum(jnp.where(hit, comb_e, 0.0), axis=1, keepdims=True)
        gate = _dot(xg, wg_ref[...])
        up = _dot(xg, wu_ref[...])
        hid = gate * _sigmoid(gate) * up * cw
        y = _dot(hid.astype(BF16), wd_ref[...]).astype(BF16)
        acc_ref[...] += _dot_tn(onehot, y)
        return carry

    lax.fori_loop(0, (cnt_ref[i, e] + R - 1) // R, block, 0)

    @pl.when(e == N_EXPERTS - 1)
    def _():
        o_ref[...] = _layer_norm(ALPHA * x_ref[...] + acc_ref[...], g_ref[...], b_ref[...])


def _moe_call(h1, comb_t, counts, lp, tm=1024):
    t, d = h1.shape
    wsp = lambda s: pl.BlockSpec((None,) + s, lambda i, e, c: (e, 0, 0))
    vec = pl.BlockSpec((1, d), lambda i, e, c: (0, 0))
    return pl.pallas_call(
        _moe_kernel,
        grid_spec=pltpu.PrefetchScalarGridSpec(
            num_scalar_prefetch=1,
            grid=(t // tm, N_EXPERTS),
            in_specs=[pl.BlockSpec((tm, d), lambda i, e, c: (i, 0)),
                      pl.BlockSpec((N_EXPERTS, tm), lambda i, e, c: (0, i)),
                      wsp((d, D_EXPERT)), wsp((d, D_EXPERT)), wsp((D_EXPERT, d)), vec, vec],
            out_specs=pl.BlockSpec((tm, d), lambda i, e, c: (i, 0)),
            scratch_shapes=[pltpu.VMEM((tm, d), BF16), pltpu.VMEM((tm, d), F32),
                            pltpu.VMEM((N_EXPERTS, tm), F32)]),
        out_shape=jax.ShapeDtypeStruct((t, d), F32),
        compiler_params=_cparams(("parallel", "arbitrary")),
        name="moe",
    )(counts, h1, comb_t, lp["w_gate"], lp["w_up"], lp["w_down"], lp["ln2_g"], lp["ln2_b"])


def _pad_rows(w, start, total):
    return jnp.zeros((total, w.shape[1]), w.dtype).at[start:start + w.shape[0]].set(w)


def _rotary_tables(positions):
    half = ROT_DIM // 2
    inv_freq = ROPE_THETA ** (-jnp.arange(0, ROT_DIM, 2, dtype=F32) / ROT_DIM)
    ang = positions.astype(F32).reshape(-1, 1) * inv_freq
    cos, sin = jnp.cos(ang), jnp.sin(ang)
    t = ang.shape[0]
    pad = HEAD_DIM - ROT_DIM
    rc = jnp.concatenate([cos, cos, jnp.ones((t, pad), F32)], axis=1)
    rm = jnp.concatenate([-sin, jnp.zeros((t, half + pad), F32)], axis=1)
    rp = jnp.concatenate([jnp.zeros((t, half), F32), sin, jnp.zeros((t, pad), F32)], axis=1)
    return tuple(jnp.tile(x, (1, HEADS_PER_VREG)) for x in (rc, rm, rp))


def _pack_in_proj(w):
    d = D_MODEL
    n_lora = DECAY_LORA + AAA_LORA + GATE_LORA
    gate, conv, rkv = w[:, :3 * d], w[:, 3 * d:6 * d], w[:, 6 * d:9 * d]
    lora = jnp.pad(w[:, 9 * d:9 * d + n_lora], ((0, 0), (0, LORA_PAD - n_lora)))
    attn = w[:, 9 * d + n_lora:]
    q, k, v = attn[:, :ATTN_W], attn[:, ATTN_W:2 * ATTN_W], attn[:, 2 * ATTN_W:]
    vg = [v[:, gi * ATTN_GW:(gi + 1) * ATTN_GW] for gi in range(len(ATTN_GROUPS))]
    cat = lambda parts: jnp.concatenate(parts, axis=1).astype(BF16)
    return {"w_pg": cat([gate, lora]), "w_pc": cat([conv, vg[0]]), "w_pr": cat([rkv, vg[1]]),
            "w_pa": cat([q, k, vg[2]])}


def kernel(x, positions, ln_in_g, ln_in_b, w_in, conv_w, rwkv_mu, rwkv_w0, rwkv_w2, rwkv_a0, rwkv_a2, rwkv_g2, rwkv_v0, rwkv_v1, rwkv_v2, rwkv_k_k, rwkv_k_a, rwkv_r_k, rwkv_lnx_g, rwkv_lnx_b, p_a, p_b, p_c, w_o, ln1_g, ln1_b, w_router, router_bias, w_gate, w_up, w_down, ln2_g, ln2_b):
    batch, seq, d = x.shape
    t = batch * seq
    depth = w_in.shape[0]
    assert d == D_MODEL and seq % (ATTN_QB * ATTN_BLOCK * ATTN_GROUPS[-1][1]) == 0
    rot_tabs = _rotary_tables(positions)
    hid = lax.broadcasted_iota(jnp.int32, (LANES, LANES), 0) // HEAD_DIM
    seg = (hid == hid.T).astype(BF16)
    w_router_t = w_router.T.astype(F32)
    rbias = router_bias.reshape(N_EXPERTS, 1).astype(F32)
    vec = lambda a: a.reshape(1, -1).astype(F32)

    h = _ln_call(x.reshape(t, d), ln_in_g, ln_in_b)
    v_first = None
    for l in range(depth):
        lp = {
            "conv_w": conv_w[l],
            "mu_r": vec(rwkv_mu[l][:3 * d]),
            "mu_l": vec(jnp.pad(rwkv_mu[l][3 * d:], (0, LORA_PAD - 288))),
            "w0": vec(rwkv_w0[l]),
            "w2": _pad_rows(rwkv_w2[l], 0, LORA_PAD).astype(BF16),
            "a0": vec(rwkv_a0[l]),
            "a2": _pad_rows(rwkv_a2[l], DECAY_LORA, LORA_PAD).astype(BF16),
            "g2": _pad_rows(rwkv_g2[l], DECAY_LORA + AAA_LORA, LORA_PAD).astype(BF16),
            "k_k": vec(rwkv_k_k[l]), "k_a": vec(rwkv_k_a[l]), "r_k": vec(rwkv_r_k[l]),
            "seg": seg,
            "p_a": p_a[l].astype(BF16), "p_b": p_b[l].astype(BF16), "p_c": p_c[l].astype(BF16),
            "w_o": w_o[l].astype(BF16), "ln1_g": vec(ln1_g[l]), "ln1_b": vec(ln1_b[l]),
            "w_gate": w_gate[l].astype(BF16), "w_up": w_up[l].astype(BF16), "w_down": w_down[l].astype(BF16),
            "ln2_g": vec(ln2_g[l]), "ln2_b": vec(ln2_b[l]),
        }
        if l > 0:
            lp["v0"] = vec(rwkv_v0[l - 1])
            lp["v1"] = jnp.pad(rwkv_v1[l - 1], ((0, 0), (0, MV_PAD - MV_LORA))).astype(BF16)
            lp["v2"] = _pad_rows(rwkv_v2[l - 1], 0, MV_PAD).astype(BF16)
        lp.update(_pack_in_proj(w_in[l]))
        gates, lora, conv, zr, q_g, k_g, v_g = _in_proj(h, lp, rot_tabs, batch, seq)
        prep = _prep_call(zr, lora, lp, h if l > 0 else None, v_first, seq)
        r, lw, k, v, a, b, g, bonus = prep[:8]
        if l == 0:
            v_first = prep[8]
        yb = _scan_call(r, lw, k, v, a, b, bonus, g, vec(rwkv_lnx_g[l]), vec(rwkv_lnx_b[l]), batch, seq)
        outs, lses = [], []
        for gi, (window, dilation) in enumerate(ATTN_GROUPS):
            o, s = _attn_call(q_g[gi], k_g[gi], v_g[gi], window, dilation)
            outs.append(o)
            lses.append(s)
        h1, comb_t, cnt = _out_call(conv, yb, outs, lses, gates, h, lp, w_router_t, rbias, seq)
        counts = cnt[:, :, 0].reshape(t // MOE_TM, -1, N_EXPERTS).sum(axis=1).astype(jnp.int32)
        h = _moe_call(h1, comb_t, counts, lp, MOE_TM)
    return h.reshape(batch, seq, d)
```

```python
import functools

import jax
import jax.numpy as jnp
import numpy as np
from jax import lax
from jax.experimental import pallas as pl
from jax.experimental.pallas import tpu as pltpu

F32 = jnp.float32
BF16 = jnp.bfloat16

D_MODEL = 1024
HEAD_DIM = 64
N_HEADS = D_MODEL // HEAD_DIM
LANES = 128
HEADS_PER_VREG = LANES // HEAD_DIM
N_PAIRS = D_MODEL // LANES
DECAY_LORA = 64
AAA_LORA = 64
GATE_LORA = 160
MV_LORA = 32
LORA_PAD = 384
MV_PAD = 128
GN_EPS = 1e-5 * HEAD_DIM
ATTN_GROUPS = ((128, 1), (512, 4), (2048, 16))
ATTN_HEADS = 8
ATTN_GW = ATTN_HEADS * HEAD_DIM
ATTN_W = len(ATTN_GROUPS) * ATTN_GW
ATTN_BLOCK = 128
ATTN_QB = 2
ROT_DIM = 16
ROPE_THETA = 500000.0
N_EXPERTS = 16
EXPERTS_PER_GROUP = 4
D_EXPERT = 512
DEPTH = 2
ALPHA = (2 * DEPTH) ** 0.25
LN_EPS = 1e-5
NEG_INF = -1e30
SCAN_CHUNK = 64
SCAN_SUB = 4
OUT_SPLIT = 2
PREP_TM = 512
MOE_TM = 1024
MOE_ROWS = 176

VMEM_LIMIT = 56 * 1024 * 1024


def _cparams(sem):
    return pltpu.CompilerParams(dimension_semantics=sem, vmem_limit_bytes=VMEM_LIMIT)


def _layer_norm(x, g, b):
    mu = jnp.mean(x, axis=-1, keepdims=True)
    xc = x - mu
    var = jnp.mean(xc * xc, axis=-1, keepdims=True)
    return xc * lax.rsqrt(var + LN_EPS) * g + b


def _sigmoid(x):
    return 1.0 / (1.0 + jnp.exp(-x))


def _dot(a, b):
    return jnp.dot(a, b, preferred_element_type=F32)


def _dot_nt(a, b):
    return lax.dot_general(a, b, (((1,), (1,)), ((), ())), preferred_element_type=F32)


def _dot_tn(a, b):
    return lax.dot_general(a, b, (((0,), (0,)), ((), ())), preferred_element_type=F32)


def _ln_kernel(x_ref, g_ref, b_ref, o_ref):
    o_ref[...] = _layer_norm(x_ref[...], g_ref[...], b_ref[...])


def _ln_call(x, g, b, tm=512):
    t, d = x.shape
    return pl.pallas_call(
        _ln_kernel,
        grid=(t // tm,),
        in_specs=[pl.BlockSpec((tm, d), lambda i: (i, 0)),
                  pl.BlockSpec((1, d), lambda i: (0, 0)),
                  pl.BlockSpec((1, d), lambda i: (0, 0))],
        out_specs=pl.BlockSpec((tm, d), lambda i: (i, 0)),
        out_shape=jax.ShapeDtypeStruct((t, d), F32),
        compiler_params=_cparams(("parallel",)),
        name="ln_in",
    )(x, g.reshape(1, d), b.reshape(1, d))


def _shift_rows(x, carry8, n):
    rolled = pltpu.roll(x, n, axis=0)
    cr = pltpu.roll(carry8, n, axis=0)
    row = lax.broadcasted_iota(jnp.int32, (8, x.shape[1]), 0)
    top = jnp.where(row < n, cr, rolled[:8])
    return jnp.concatenate([top, rolled[8:]], axis=0)


def _store_residue_major(dst, tg, dil, c_rows):
    tm = tg.shape[0]
    if dil == 1:
        dst[0] = tg.astype(dst.dtype)
        return
    nl = ATTN_GW // LANES
    for j in range(nl):
        c_rows[j] = tg[:, j * LANES:(j + 1) * LANES]
    for res in range(dil):
        for j in range(nl):
            dst[res, :, j * LANES:(j + 1) * LANES] = (
                c_rows[j, pl.ds(res, tm // dil, stride=dil), :].astype(dst.dtype))


def _proj_gate_kernel(x_ref, w_ref, gate_o):
    gate_o[...] = _sigmoid(_dot(x_ref[...].astype(BF16), w_ref[...])).astype(gate_o.dtype)


def _proj_conv_kernel(tiles_per_seq, x_ref, w_ref, convw_ref, conv_o, v0_o, carry):
    i = pl.program_id(0)

    @pl.when(i % tiles_per_seq == 0)
    def _():
        carry[...] = jnp.zeros_like(carry)

    acc = _dot(x_ref[...].astype(BF16), w_ref[...])
    tm = acc.shape[0]
    d = D_MODEL
    ch = acc[:, d:2 * d] * acc[:, 2 * d:3 * d]
    cr = carry[...]
    conv = (convw_ref[0:1, :] * _shift_rows(ch, cr, 2)
            + convw_ref[1:2, :] * _shift_rows(ch, cr, 1)
            + convw_ref[2:3, :] * ch)
    conv_o[...] = (acc[:, :d] * conv).astype(conv_o.dtype)
    carry[...] = ch[tm - 8:, :]
    v0_o[0] = acc[:, 3 * d:].astype(v0_o.dtype)


def _proj_attn_kernel(dil_v, x_ref, w_ref, rc_ref, rm_ref, rp_ref, *rest):
    n_g = len(ATTN_GROUPS)
    q_o, k_o, v_o, c_rows = rest[:n_g], rest[n_g:2 * n_g], rest[2 * n_g], rest[2 * n_g + 1]
    xb = x_ref[...].astype(BF16)
    reps = ATTN_W // LANES
    rc = jnp.concatenate([rc_ref[...]] * reps, axis=1)
    rm = jnp.concatenate([rm_ref[...]] * reps, axis=1)
    rp = jnp.concatenate([rp_ref[...]] * reps, axis=1)
    half = ROT_DIM // 2

    def rotate_store(t, dsts, scale):
        t = (t * rc + pltpu.roll(t, ATTN_W - half, axis=1) * rm + pltpu.roll(t, half, axis=1) * rp) * scale
        for gi, (_, dil) in enumerate(ATTN_GROUPS):
            _store_residue_major(dsts[gi], t[:, gi * ATTN_GW:(gi + 1) * ATTN_GW], dil, c_rows)

    tq = _dot(xb, w_ref[:, 0:ATTN_W])
    tk = _dot(xb, w_ref[:, ATTN_W:2 * ATTN_W])
    rotate_store(tq, q_o, HEAD_DIM ** -0.5)
    tv = _dot(xb, w_ref[:, 2 * ATTN_W:])
    rotate_store(tk, k_o, 1.0)
    _store_residue_major(v_o, tv, dil_v, c_rows)


def _proj_call(kernel, name, h, w, extra, extra_specs, out_specs, out_shape, scratch, sem, tm):
    t, d = h.shape
    return pl.pallas_call(
        kernel,
        grid=(t // tm,),
        in_specs=[pl.BlockSpec((tm, d), lambda i: (i, 0)), pl.BlockSpec(w.shape, lambda i: (0, 0))] + extra_specs,
        out_specs=out_specs,
        out_shape=out_shape,
        scratch_shapes=scratch,
        compiler_params=_cparams((sem,)),
        name=name,
    )(h, w, *extra)


def _in_proj(h, lp, rot_tabs, batch, seq, tm=512):
    t, d = h.shape
    tps = seq // tm
    row = lambda w: pl.BlockSpec((tm, w), lambda i: (i, 0))
    res_spec = lambda dil: pl.BlockSpec((None, dil, tm // dil, ATTN_GW), lambda i: (i // tps, 0, i % tps, 0))
    res_shape = lambda dil: jax.ShapeDtypeStruct((batch, dil, seq // dil, ATTN_GW), BF16)
    rows_scratch = pltpu.VMEM((ATTN_GW // LANES, tm, LANES), F32)
    dils = [dil for _, dil in ATTN_GROUPS]
    gates, = _proj_call(
        _proj_gate_kernel, "proj_gate", h, lp["w_pg"], [], [],
        [row(3 * d)], [jax.ShapeDtypeStruct((t, 3 * d), BF16)], [], "parallel", tm)
    conv, v0 = _proj_call(
        functools.partial(_proj_conv_kernel, tps), "proj_conv", h, lp["w_pc"], [lp["conv_w"]],
        [pl.BlockSpec(lp["conv_w"].shape, lambda i: (0, 0))],
        [row(d), res_spec(dils[0])],
        [jax.ShapeDtypeStruct((t, d), BF16), res_shape(dils[0])],
        [pltpu.VMEM((8, d), F32)], "arbitrary", tm)
    qkv = _proj_call(
        functools.partial(_proj_attn_kernel, dils[2]), "proj_attn", h, lp["w_pa"], list(rot_tabs),
        [row(LANES)] * 3,
        [res_spec(dl) for dl in dils] * 2 + [res_spec(dils[2])],
        [res_shape(dl) for dl in dils] * 2 + [res_shape(dils[2])],
        [rows_scratch], "parallel", tm)
    n_g = len(ATTN_GROUPS)
    q, k, v2 = qkv[:n_g], qkv[n_g:2 * n_g], qkv[2 * n_g]
    return gates, conv, q, k, v0, v2


def _prep_kernel(has_vres, tiles_per_seq, dil,
                 x_ref, w_ref, mu_r_ref, mu_l_ref, w0_ref, w2_ref, a0_ref, a2_ref, g2_ref,
                 kk_ref, ka_ref, rk_ref, seg_ref, *rest):
    if has_vres:
        vf_ref, v0_ref, v1_ref, v2_ref = rest[:4]
        rest = rest[4:]
    r_o, lw_o, k_o, v_o, a_o, b_o, g_o, bonus_o = rest[:8]
    rest = rest[8:]
    if not has_vres:
        vf_o = rest[0]
        rest = rest[1:]
    vattn_o, c_zr, c_zl, c_rows = rest
    i = pl.program_id(0)

    @pl.when(i % tiles_per_seq == 0)
    def _():
        c_zr[...] = jnp.zeros_like(c_zr)
        c_zl[...] = jnp.zeros_like(c_zl)

    tm = x_ref.shape[0]
    d = D_MODEL
    xb = x_ref[...].astype(BF16)
    acc = _dot(xb, w_ref[...])
    _store_residue_major(vattn_o, acc[:, 3 * d + LORA_PAD:], dil, c_rows)
    zr = acc[:, :3 * d]
    zl = acc[:, 3 * d:3 * d + LORA_PAD]
    zr_m = zr + (_shift_rows(zr, c_zr[...], 1) - zr) * mu_r_ref[...]
    zl_m = zl + (_shift_rows(zl, c_zl[...], 1) - zl) * mu_l_ref[...]
    c_zr[...] = zr[tm - 8:, :]
    c_zl[...] = zl[tm - 8:, :]
    r = zr_m[:, :d]
    k = zr_m[:, d:2 * d]
    v = zr_m[:, 2 * d:]
    zl_wa = zl_m[:, :LANES]
    zl_g = zl_m[:, LANES:]
    x = w0_ref[...] + _dot(jnp.tanh(zl_wa).astype(BF16), w2_ref[...])
    lw_o[...] = -float(np.exp(-0.5)) * _sigmoid(x)
    if has_vres:
        lo = _dot(xb, v1_ref[...])
        gate_v = _sigmoid(v0_ref[...] + _dot(lo.astype(BF16), v2_ref[...]))
        v = v + (vf_ref[...].astype(F32) - v) * gate_v
    else:
        vf_o[...] = v.astype(vf_o.dtype)
    a_gate = _sigmoid(a0_ref[...] + _dot(zl_wa.astype(BF16), a2_ref[...]))
    g_o[...] = _dot(_sigmoid(zl_g).astype(BF16), g2_ref[...]).astype(g_o.dtype)
    seg = seg_ref[...]

    def head_sums(x):
        xb = x.astype(BF16)
        return jnp.concatenate([_dot(xb[:, p * LANES:(p + 1) * LANES], seg) for p in range(N_PAIRS)], axis=1)

    kk = k * kk_ref[...]
    kk = kk * lax.rsqrt(jnp.maximum(head_sums(kk * kk), 1e-24))
    k2 = k * (1.0 + (a_gate - 1.0) * ka_ref[...])
    bonus = head_sums(r * k2 * rk_ref[...]) * v
    r_o[...] = r.astype(r_o.dtype)
    k_o[...] = k2.astype(k_o.dtype)
    v_o[...] = v.astype(v_o.dtype)
    a_o[...] = (-kk).astype(a_o.dtype)
    b_o[...] = (kk * a_gate).astype(b_o.dtype)
    bonus_o[...] = bonus.astype(bonus_o.dtype)


def _prep_call(h, lp, v_first, dil, batch, seq, tm=PREP_TM):
    t, d = h.shape
    has_vres = v_first is not None
    tps = seq // tm
    row = lambda w: pl.BlockSpec((tm, w), lambda i: (i, 0))
    full = lambda a: pl.BlockSpec(a.shape, lambda i: (0,) * a.ndim)
    consts = [lp["w_pr"], lp["mu_r"], lp["mu_l"], lp["w0"], lp["w2"], lp["a0"], lp["a2"], lp["g2"],
              lp["k_k"], lp["k_a"], lp["r_k"], lp["seg"]]
    args = [h, *consts]
    in_specs = [row(d)] + [full(a) for a in consts]
    if has_vres:
        extra = [v_first, lp["v0"], lp["v1"], lp["v2"]]
        args += extra
        in_specs += [row(d)] + [full(a) for a in extra[1:]]
    out_dt = [BF16, F32, BF16, BF16, BF16, BF16, BF16, BF16] + ([] if has_vres else [BF16])
    return pl.pallas_call(
        functools.partial(_prep_kernel, has_vres, tps, dil),
        grid=(t // tm,),
        in_specs=in_specs,
        out_specs=[row(d) for _ in out_dt]
                  + [pl.BlockSpec((None, dil, tm // dil, ATTN_GW), lambda i: (i // tps, 0, i % tps, 0))],
        out_shape=[jax.ShapeDtypeStruct((t, d), dt) for dt in out_dt]
                  + [jax.ShapeDtypeStruct((batch, dil, seq // dil, ATTN_GW), BF16)],
        scratch_shapes=[pltpu.VMEM((8, 3 * d), F32), pltpu.VMEM((8, LORA_PAD), F32),
                        pltpu.VMEM((ATTN_GW // LANES, tm, LANES), F32)],
        compiler_params=_cparams(("arbitrary",)),
        name="proj_rwkv",
    )(*args)


def _scan_kernel(r_ref, lw_ref, k_ref, v_ref, a_ref, b_ref, bonus_ref, g_ref, lng_ref, lnb_ref,
                 o_ref, h_ref):
    c = pl.program_id(1)

    @pl.when(c == 0)
    def _():
        h_ref[...] = jnp.zeros_like(h_ref)

    C = SCAN_CHUNK
    n2 = HEADS_PER_VREG * C
    ri = lax.broadcasted_iota(jnp.int32, (C, C), 0)
    ci = lax.broadcasted_iota(jnp.int32, (C, C), 1)
    tri = (ri >= ci).astype(BF16)
    lane = lax.broadcasted_iota(jnp.int32, (1, LANES), 1)
    head_masks = [((lane // HEAD_DIM) == hh).astype(F32) for hh in range(HEADS_PER_VREG)]
    r2 = lax.broadcasted_iota(jnp.int32, (n2, 2 * n2), 0)
    c2 = lax.broadcasted_iota(jnp.int32, (n2, 2 * n2), 1)
    same = (r2 // C) == ((c2 % n2) // C)
    strict = same & ((r2 % C) > (c2 % C))
    incl = same & ((r2 % C) >= (c2 % C))
    strict_k = strict & (c2 >= n2)
    wn = max(n2, LANES)
    rw = lax.broadcasted_iota(jnp.int32, (n2, wn), 0)
    cw = lax.broadcasted_iota(jnp.int32, (n2, wn), 1)
    strict_b = ((rw // C) == (cw // C)) & ((rw % C) > (cw % C)) & (cw < n2)
    eye = (rw == cw).astype(F32)
    zero_rows = jnp.zeros((n2, LANES), BF16)

    def pad_rows(x):
        return x if wn == n2 else jnp.concatenate([x, jnp.zeros((wn - n2, x.shape[1]), x.dtype)], axis=0)
    lr = lax.broadcasted_iota(jnp.int32, (LANES, LANES), 0)
    lc = lax.broadcasted_iota(jnp.int32, (LANES, LANES), 1)
    seg_mean = jnp.where((lr // HEAD_DIM) == (lc // HEAD_DIM), 1.0 / HEAD_DIM, 0.0).astype(BF16)

    def stack(x):
        return jnp.concatenate([x * m for m in head_masks], axis=0).astype(BF16)

    n_sub = r_ref.shape[0] // C
    pairs = range(N_PAIRS)
    sls = [slice(p * LANES, (p + 1) * LANES) for p in pairs]
    items = [(s, p) for s in range(n_sub) for p in pairs]
    cums = []
    for s in range(n_sub):
        lw_s = lw_ref[s * C:(s + 1) * C, :]
        lw_hi = lw_s.astype(BF16)
        rem = lw_s - lw_hi.astype(F32)
        lw_mid = rem.astype(BF16)
        lw_lo = (rem - lw_mid.astype(F32)).astype(BF16)
        cums.append(_dot(tri, lw_hi) + (_dot(tri, lw_mid) + _dot(tri, lw_lo)))
    lhs, rhs, bk, v2, dec = [], [], [], [], []
    for s, p in items:
        rows, sl = slice(s * C, (s + 1) * C), sls[p]
        lw = lw_ref[rows, sl]
        cum = cums[s][:, sl]
        tot = cum[C - 1:C, :]
        e_bwd = jnp.exp(-cum)
        e_rest = jnp.exp(tot - cum)
        r = r_ref[rows, sl].astype(F32)
        k = k_ref[rows, sl].astype(F32)
        a = a_ref[rows, sl].astype(F32)
        b = b_ref[rows, sl].astype(F32)
        lhs.append(jnp.concatenate([stack(a * jnp.exp(cum - lw)), stack(r * jnp.exp(cum))], axis=0))
        rhs.append(jnp.concatenate([stack(b * e_bwd), stack(k * e_bwd)], axis=0))
        bk.append(jnp.concatenate([stack(b * e_rest), stack(k * e_rest)], axis=0))
        v2.append(stack(v_ref[rows, sl].astype(F32)))
        dec.append(jnp.exp(tot))
    amat = [_dot_nt(l, rr) for l, rr in zip(lhs, rhs)]
    a_ab = [jnp.where(strict_b, m[:n2, :wn], 0.0) for m in amat]
    a_ak = [jnp.where(strict_k, m[:n2, :], 0.0).astype(BF16) for m in amat]
    a_r = [jnp.where(incl, m[n2:, :], 0.0).astype(BF16) for m in amat]
    tinv = [eye + m for m in a_ab]
    pb = [m.astype(BF16) for m in a_ab]
    for _ in range(int(np.log2(C)) - 1):
        pb = [_dot(q, pad_rows(q)).astype(BF16) for q in pb]
        tinv = [t + _dot(q, pad_rows(t.astype(BF16))) for q, t in zip(pb, tinv)]
    ht = [h_ref[p] for p in pairs]
    ys = []
    for s in range(n_sub):
        it = [s * N_PAIRS + p for p in pairs]
        ah = [_dot_nt(lhs[i], ht[p].astype(BF16)) for p, i in zip(pairs, it)]
        w = [ah[p][:n2] + _dot(a_ak[i], jnp.concatenate([zero_rows, v2[i]], axis=0)) for p, i in zip(pairs, it)]
        u = [_dot(tinv[i].astype(BF16), pad_rows(w[p].astype(BF16))) for p, i in zip(pairs, it)]
        uv = [jnp.concatenate([u[p].astype(BF16), v2[i]], axis=0) for p, i in zip(pairs, it)]
        y2 = [ah[p][n2:] + _dot(a_r[i], uv[p]) for p, i in zip(pairs, it)]
        ht = [ht[p] * dec[i] + _dot_tn(uv[p], bk[i]) for p, i in zip(pairs, it)]
        for m in y2:
            y = m[:C]
            for hh in range(1, HEADS_PER_VREG):
                y = y + m[hh * C:(hh + 1) * C]
            ys.append(y)
    for p in pairs:
        h_ref[p] = ht[p]
    y = jnp.concatenate(ys, axis=0)
    y_hi = y.astype(BF16)
    ym = _dot(y_hi, seg_mean) + _dot((y - y_hi.astype(F32)).astype(BF16), seg_mean)
    yc = y - ym
    yv = _dot((yc * yc).astype(BF16), seg_mean)
    yn = yc * lax.rsqrt(yv + GN_EPS)
    for i, (s, p) in enumerate(items):
        rows, sl = slice(s * C, (s + 1) * C), sls[p]
        out = yn[i * C:(i + 1) * C] * lng_ref[:, sl] + lnb_ref[:, sl]
        out = (out + bonus_ref[rows, sl].astype(F32)) * g_ref[rows, sl].astype(F32)
        o_ref[rows, sl] = out.astype(o_ref.dtype)


def _scan_call(r, lw, k, v, a, b, bonus, g, lng, lnb, batch, seq):
    t, d = r.shape
    rows = SCAN_SUB * SCAN_CHUNK
    nc = seq // rows
    blk = pl.BlockSpec((rows, d), lambda bi, ci: (bi * nc + ci, 0))
    full = pl.BlockSpec((1, d), lambda bi, ci: (0, 0))
    return pl.pallas_call(
        _scan_kernel,
        grid=(batch, nc),
        in_specs=[blk] * 8 + [full, full],
        out_specs=blk,
        out_shape=jax.ShapeDtypeStruct((t, d), BF16),
        scratch_shapes=[pltpu.VMEM((N_PAIRS, LANES, LANES), F32)],
        compiler_params=_cparams(("parallel", "arbitrary")),
        name="wkv7_scan",
    )(r, lw, k, v, a, b, bonus, g, lng, lnb)


def _attn_kernel(span, q_ref, kp_ref, kc_ref, vp_ref, vc_ref, o_ref, lse_ref):
    n = pl.program_id(2)
    Q = ATTN_BLOCK
    qi = lax.broadcasted_iota(jnp.int32, (Q, 2 * Q), 0)
    kj = lax.broadcasted_iota(jnp.int32, (Q, 2 * Q), 1)
    dist = Q + qi - kj
    band = (dist >= 0) & (dist <= span)
    first = band & ((n > 0) | (kj >= Q))
    lane = lax.broadcasted_iota(jnp.int32, (1, LANES), 1)
    masks = [(lane // HEAD_DIM) == hh for hh in range(HEADS_PER_VREG)]
    sls = [slice(p * LANES, (p + 1) * LANES) for p in range(ATTN_GW // LANES)]
    qbs = range(ATTN_QB)

    def keys(prev_ref, cur_ref, j, sl):
        lo = prev_ref[:, sl] if j == 0 else cur_ref[(j - 1) * Q:j * Q, sl]
        return jnp.concatenate([lo, cur_ref[j * Q:(j + 1) * Q, sl]], axis=0)

    chains = [(j, sl, m) for j in qbs for sl in sls for m in masks]
    scores = []
    for j, sl, m in chains:
        q = q_ref[j * Q:(j + 1) * Q, sl]
        scores.append(_dot_nt(jnp.where(m, q, jnp.zeros_like(q)), keys(kp_ref, kc_ref, j, sl)))
    probs, lses = [], []
    for (j, _, _), s in zip(chains, scores):
        s = jnp.where(first if j == 0 else band, s, NEG_INF)
        mx = jnp.max(s, axis=-1, keepdims=True)
        e = jnp.exp(s - mx)
        den = jnp.sum(e, axis=-1, keepdims=True)
        probs.append((e / den).astype(BF16))
        lses.append(mx + jnp.log(den))
    nh = len(masks)
    for c0 in range(0, len(chains), nh):
        j, sl, _ = chains[c0]
        vv = keys(vp_ref, vc_ref, j, sl)
        out = None
        lse = None
        for hh, m in enumerate(masks):
            o = _dot(probs[c0 + hh], jnp.where(m, vv, jnp.zeros_like(vv)))
            l = lses[c0 + hh]
            out = o if out is None else out + o
            lse = jnp.broadcast_to(l, (Q, LANES)) if lse is None else jnp.where(m, l, lse)
        o_ref[j * Q:(j + 1) * Q, sl] = out.astype(o_ref.dtype)
        lse_ref[j * Q:(j + 1) * Q, sl] = lse


def _attn_call(q, k, v, window, dilation):
    batch, _, L, _ = q.shape
    Q = ATTN_BLOCK
    nb = L // (ATTN_QB * Q)
    cur = pl.BlockSpec((None, None, ATTN_QB * Q, ATTN_GW), lambda b, r, n: (b, r, n, 0))
    prev = pl.BlockSpec((None, None, Q, ATTN_GW),
                        lambda b, r, n: (b, r, jnp.maximum(ATTN_QB * n - 1, 0), 0))
    return pl.pallas_call(
        functools.partial(_attn_kernel, window // dilation),
        grid=(batch, dilation, nb),
        in_specs=[cur, prev, cur, prev, cur],
        out_specs=[cur, cur],
        out_shape=[jax.ShapeDtypeStruct(q.shape, BF16), jax.ShapeDtypeStruct(q.shape, F32)],
        compiler_params=_cparams(("parallel", "parallel", "arbitrary")),
        name=f"dilated_attn_d{dilation}",
    )(q, k, k, v, v)


def _route(logits, bias):
    mxl = jnp.max(logits, axis=0, keepdims=True)
    ex = jnp.exp(logits - mxl)
    probs = ex / jnp.sum(ex, axis=0, keepdims=True)
    sel = probs + bias
    rows = [sel[e:e + 1, :] for e in range(N_EXPERTS)]
    n_groups = N_EXPERTS // EXPERTS_PER_GROUP
    scores = []
    for gi in range(n_groups):
        a, b, c, dd = rows[EXPERTS_PER_GROUP * gi:EXPERTS_PER_GROUP * (gi + 1)]
        hi1, lo1 = jnp.maximum(a, b), jnp.minimum(a, b)
        hi2, lo2 = jnp.maximum(c, dd), jnp.minimum(c, dd)
        scores.append(jnp.maximum(hi1, hi2) + jnp.maximum(jnp.minimum(hi1, hi2), jnp.maximum(lo1, lo2)))
    best_score = scores[0]
    best = jnp.zeros_like(scores[0], dtype=jnp.int32)
    for gi in range(1, n_groups):
        better = scores[gi] > best_score
        best = jnp.where(better, gi, best)
        best_score = jnp.where(better, scores[gi], best_score)
    eid = lax.broadcasted_iota(jnp.int32, sel.shape, 0)
    masked = jnp.where((eid // EXPERTS_PER_GROUP) == best, sel, -jnp.inf)
    m1 = jnp.max(masked, axis=0, keepdims=True)
    i1 = jnp.min(jnp.where(masked == m1, eid, N_EXPERTS), axis=0, keepdims=True)
    masked2 = jnp.where(eid == i1, -jnp.inf, masked)
    m2 = jnp.max(masked2, axis=0, keepdims=True)
    i2 = jnp.min(jnp.where(masked2 == m2, eid, N_EXPERTS), axis=0, keepdims=True)
    p1 = jnp.sum(jnp.where(eid == i1, probs, 0.0), axis=0, keepdims=True)
    p2 = jnp.sum(jnp.where(eid == i2, probs, 0.0), axis=0, keepdims=True)
    tot = p1 + p2
    return jnp.where(eid == i1, p1 / tot, 0.0) + jnp.where(eid == i2, p2 / tot, 0.0)


def _out_kernel(conv_ref, yb_ref, o0_ref, o1_ref, o2_ref, l0_ref, l1_ref, l2_ref, zg_ref, h_ref,
                pa_ref, pb_ref, pc_ref, wo_ref, g_ref, b_ref, wr_ref, rb_ref, h1_ref, comb_ref, cnt_ref, sc_ref):
    d = D_MODEL
    tm = h_ref.shape[0]
    hm = tm // OUT_SPLIT
    nl = ATTN_GW // LANES
    dils = [dil for _, dil in ATTN_GROUPS]
    slot = iter(range(sc_ref.shape[0]))

    def token_order(ref, dil, s):
        n = hm // dil
        if dil == 1:
            return ref[0, s * hm:(s + 1) * hm, :].astype(F32)
        buf = next(slot)
        for res in range(dil):
            blk = ref[res, s * n:(s + 1) * n, :].astype(F32)
            for j in range(nl):
                sc_ref[buf, j, pl.ds(res, n, stride=dil), :] = blk[:, j * LANES:(j + 1) * LANES]
        return jnp.concatenate([sc_ref[buf, j] for j in range(nl)], axis=1)

    subs = range(OUT_SPLIT)
    rows = [slice(s * hm, (s + 1) * hm) for s in subs]
    ya, yb, ycs = [], [], []
    for s, r in enumerate(rows):
        ya.append(_dot(conv_ref[r, :], pa_ref[...]))
        l0, l1, l2 = [token_order(rf, dl, s) for rf, dl in zip((l0_ref, l1_ref, l2_ref), dils)]
        mx = jnp.maximum(jnp.maximum(l0, l1), l2)
        e0, e1, e2 = jnp.exp(l0 - mx), jnp.exp(l1 - mx), jnp.exp(l2 - mx)
        yb.append(_dot(yb_ref[r, :], pb_ref[...]))
        yc = e0 * token_order(o0_ref, dils[0], s)
        yc = yc + e1 * token_order(o1_ref, dils[1], s)
        yc = (yc + e2 * token_order(o2_ref, dils[2], s)) / (e0 + e1 + e2)
        ycs.append(yc.astype(BF16))
    yc_full = _dot(jnp.concatenate(ycs, axis=0), pc_ref[...])
    yc = [yc_full[r, :] for r in rows]
    mixes = []
    for s, r in enumerate(rows):
        ga = zg_ref[r, 0:d].astype(F32)
        gb = zg_ref[r, d:2 * d].astype(F32)
        gc = zg_ref[r, 2 * d:3 * d].astype(F32)
        mixes.append((ga * ya[s] + gb * yb[s] + gc * yc[s]).astype(BF16))
    outs = [_dot(m, wo_ref[...]) for m in mixes]
    wr = wr_ref[...]
    wr_hi = wr.astype(BF16)
    wr_hl = jnp.concatenate([wr_hi, (wr - wr_hi.astype(F32)).astype(BF16)], axis=0)
    logits = []
    for s, r in enumerate(rows):
        h1 = _layer_norm(ALPHA * h_ref[r, :] + outs[s], g_ref[...], b_ref[...])
        h1_ref[r, :] = h1
        h_hi = h1.astype(BF16)
        t1 = _dot_nt(wr_hl, h_hi)
        t2 = _dot_nt(wr_hi, (h1 - h_hi.astype(F32)).astype(BF16))
        logits.append(t1[:N_EXPERTS] + (t1[N_EXPERTS:] + t2))
    logits = jnp.concatenate(logits, axis=1)
    comb = _route(logits, rb_ref[...])
    comb_ref[...] = comb
    count = jnp.sum((comb > 0.0).astype(F32), axis=1, keepdims=True)
    cnt_ref[...] = jnp.broadcast_to(count, cnt_ref.shape)


def _out_call(conv, yb, outs, lses, z, h, lp, w_router_t, router_bias, seq, tm=512):
    t, d = h.shape
    tps = seq // tm
    row = lambda w, j=0: pl.BlockSpec((tm, w), lambda i, j=j: (i, j))
    full = lambda a: pl.BlockSpec(a.shape, lambda i: (0,) * a.ndim)
    res = [pl.BlockSpec((None, dil, tm // dil, ATTN_GW), lambda i: (i // tps, 0, i % tps, 0))
           for _, dil in ATTN_GROUPS]
    consts = [lp["p_a"], lp["p_b"], lp["p_c"], lp["w_o"], lp["ln1_g"], lp["ln1_b"], w_router_t, router_bias]
    n_relayout = 2 * OUT_SPLIT * sum(1 for _, dil in ATTN_GROUPS if dil > 1)
    return pl.pallas_call(
        _out_kernel,
        grid=(t // tm,),
        in_specs=[row(d), row(d)] + res + res + [row(3 * d), row(d)]
                 + [full(a) for a in consts],
        out_specs=[row(d), pl.BlockSpec((N_EXPERTS, tm), lambda i: (0, i)),
                   pl.BlockSpec((None, N_EXPERTS, LANES), lambda i: (i, 0, 0))],
        out_shape=[jax.ShapeDtypeStruct((t, d), F32), jax.ShapeDtypeStruct((N_EXPERTS, t), F32),
                   jax.ShapeDtypeStruct((t // tm, N_EXPERTS, LANES), F32)],
        scratch_shapes=[pltpu.VMEM((n_relayout, ATTN_GW // LANES, tm // OUT_SPLIT, LANES), F32)],
        compiler_params=_cparams(("parallel",)),
        name="mixer_out",
    )(conv, yb, *outs, *lses, z, h, *consts)


def _moe_kernel(cnt_ref, x_ref, comb_ref, wg_ref, wu_ref, wd_ref, g_ref, b_ref, o_ref,
                xb_ref, acc_ref, rank_ref):
    i = pl.program_id(0)
    e = pl.program_id(1)
    tm = x_ref.shape[0]
    R = MOE_ROWS

    @pl.when(e == 0)
    def _():
        xb_ref[...] = x_ref[...].astype(BF16)
        acc_ref[...] = jnp.zeros_like(acc_ref)
        sel = (comb_ref[...] > 0.0).astype(BF16)
        tri = (lax.broadcasted_iota(jnp.int32, (tm, tm), 0)
               <= lax.broadcasted_iota(jnp.int32, (tm, tm), 1)).astype(BF16)
        rank_ref[...] = _dot(sel, tri) - 1.0

    comb_e = comb_ref[pl.ds(e, 1), :]
    rank_e = jnp.where(comb_e > 0.0, rank_ref[pl.ds(e, 1), :], -1.0)
    slot = lax.broadcasted_iota(jnp.int32, (R, tm), 0).astype(F32)

    def block(blk, carry):
        hit = rank_e == slot + (blk * R).astype(F32)
        onehot = hit.astype(BF16)
        xg = _dot(onehot, xb_ref[...]).astype(BF16)
        cw = jnp.sum(jnp.where(hit, comb_e, 0.0), axis=1, keepdims=True)
        gate = _dot(xg, wg_ref[...])
        up = _dot(xg, wu_ref[...])
        hid = gate * _sigmoid(gate) * up * cw
        y = _dot(hid.astype(BF16), wd_ref[...]).astype(BF16)
        acc_ref[...] += _dot_tn(onehot, y)
        return carry

    lax.fori_loop(0, (cnt_ref[i, e] + R - 1) // R, block, 0)

    @pl.when(e == N_EXPERTS - 1)
    def _():
        o_ref[...] = _layer_norm(ALPHA * x_ref[...] + acc_ref[...], g_ref[...], b_ref[...])


def _moe_call(h1, comb_t, counts, lp, tm=1024):
    t, d = h1.shape
    wsp = lambda s: pl.BlockSpec((None,) + s, lambda i, e, c: (e, 0, 0))
    vec = pl.BlockSpec((1, d), lambda i, e, c: (0, 0))
    return pl.pallas_call(
        _moe_kernel,
        grid_spec=pltpu.PrefetchScalarGridSpec(
            num_scalar_prefetch=1,
            grid=(t // tm, N_EXPERTS),
            in_specs=[pl.BlockSpec((tm, d), lambda i, e, c: (i, 0)),
                      pl.BlockSpec((N_EXPERTS, tm), lambda i, e, c: (0, i)),
                      wsp((d, D_EXPERT)), wsp((d, D_EXPERT)), wsp((D_EXPERT, d)), vec, vec],
            out_specs=pl.BlockSpec((tm, d), lambda i, e, c: (i, 0)),
            scratch_shapes=[pltpu.VMEM((tm, d), BF16), pltpu.VMEM((tm, d), F32),
                            pltpu.VMEM((N_EXPERTS, tm), F32)]),
        out_shape=jax.ShapeDtypeStruct((t, d), F32),
        compiler_params=_cparams(("parallel", "arbitrary")),
        name="moe",
    )(counts, h1, comb_t, lp["w_gate"], lp["w_up"], lp["w_down"], lp["ln2_g"], lp["ln2_b"])


def _pad_rows(w, start, total):
    return jnp.zeros((total, w.shape[1]), w.dtype).at[start:start + w.shape[0]].set(w)


def _rotary_tables(positions):
    half = ROT_DIM // 2
    inv_freq = ROPE_THETA ** (-jnp.arange(0, ROT_DIM, 2, dtype=F32) / ROT_DIM)
    ang = positions.astype(F32).reshape(-1, 1) * inv_freq
    cos, sin = jnp.cos(ang), jnp.sin(ang)
    t = ang.shape[0]
    pad = HEAD_DIM - ROT_DIM
    rc = jnp.concatenate([cos, cos, jnp.ones((t, pad), F32)], axis=1)
    rm = jnp.concatenate([-sin, jnp.zeros((t, half + pad), F32)], axis=1)
    rp = jnp.concatenate([jnp.zeros((t, half), F32), sin, jnp.zeros((t, pad), F32)], axis=1)
    return tuple(jnp.tile(x, (1, HEADS_PER_VREG)) for x in (rc, rm, rp))


def _pack_in_proj(w):
    d = D_MODEL
    n_lora = DECAY_LORA + AAA_LORA + GATE_LORA
    gate, conv, rkv = w[:, :3 * d], w[:, 3 * d:6 * d], w[:, 6 * d:9 * d]
    lora = jnp.pad(w[:, 9 * d:9 * d + n_lora], ((0, 0), (0, LORA_PAD - n_lora)))
    attn = w[:, 9 * d + n_lora:]
    q, k, v = attn[:, :ATTN_W], attn[:, ATTN_W:2 * ATTN_W], attn[:, 2 * ATTN_W:]
    vg = [v[:, gi * ATTN_GW:(gi + 1) * ATTN_GW] for gi in range(len(ATTN_GROUPS))]
    cat = lambda parts: jnp.concatenate(parts, axis=1).astype(BF16)
    return {"w_pg": cat([gate]), "w_pc": cat([conv, vg[0]]), "w_pr": cat([rkv, lora, vg[1]]),
            "w_pa": cat([q, k, vg[2]])}


def kernel(x, positions, ln_in_g, ln_in_b, w_in, conv_w, rwkv_mu, rwkv_w0, rwkv_w2, rwkv_a0, rwkv_a2, rwkv_g2, rwkv_v0, rwkv_v1, rwkv_v2, rwkv_k_k, rwkv_k_a, rwkv_r_k, rwkv_lnx_g, rwkv_lnx_b, p_a, p_b, p_c, w_o, ln1_g, ln1_b, w_router, router_bias, w_gate, w_up, w_down, ln2_g, ln2_b):
    batch, seq, d = x.shape
    t = batch * seq
    depth = w_in.shape[0]
    assert d == D_MODEL and seq % (ATTN_QB * ATTN_BLOCK * ATTN_GROUPS[-1][1]) == 0
    rot_tabs = _rotary_tables(positions)
    hid = lax.broadcasted_iota(jnp.int32, (LANES, LANES), 0) // HEAD_DIM
    seg = (hid == hid.T).astype(BF16)
    w_router_t = w_router.T.astype(F32)
    rbias = router_bias.reshape(N_EXPERTS, 1).astype(F32)
    vec = lambda a: a.reshape(1, -1).astype(F32)

    h = _ln_call(x.reshape(t, d), ln_in_g, ln_in_b)
    v_first = None
    for l in range(depth):
        lp = {
            "conv_w": conv_w[l],
            "mu_r": vec(rwkv_mu[l][:3 * d]),
            "mu_l": vec(jnp.pad(rwkv_mu[l][3 * d:], (0, LORA_PAD - 288))),
            "w0": vec(rwkv_w0[l]),
            "w2": _pad_rows(rwkv_w2[l], 0, LANES).astype(BF16),
            "a0": vec(rwkv_a0[l]),
            "a2": _pad_rows(rwkv_a2[l], DECAY_LORA, LANES).astype(BF16),
            "g2": _pad_rows(rwkv_g2[l], 0, LORA_PAD - LANES).astype(BF16),
            "k_k": vec(rwkv_k_k[l]), "k_a": vec(rwkv_k_a[l]), "r_k": vec(rwkv_r_k[l]),
            "seg": seg,
            "p_a": p_a[l].astype(BF16), "p_b": p_b[l].astype(BF16), "p_c": p_c[l].astype(BF16),
            "w_o": w_o[l].astype(BF16), "ln1_g": vec(ln1_g[l]), "ln1_b": vec(ln1_b[l]),
            "w_gate": w_gate[l].astype(BF16), "w_up": w_up[l].astype(BF16), "w_down": w_down[l].astype(BF16),
            "ln2_g": vec(ln2_g[l]), "ln2_b": vec(ln2_b[l]),
        }
        if l > 0:
            lp["v0"] = vec(rwkv_v0[l - 1])
            lp["v1"] = jnp.pad(rwkv_v1[l - 1], ((0, 0), (0, MV_PAD - MV_LORA))).astype(BF16)
            lp["v2"] = _pad_rows(rwkv_v2[l - 1], 0, MV_PAD).astype(BF16)
        lp.update(_pack_in_proj(w_in[l]))
        gates, conv, q_g, k_g, v_g0, v_g2 = _in_proj(h, lp, rot_tabs, batch, seq)
        prep = _prep_call(h, lp, v_first, ATTN_GROUPS[1][1], batch, seq)
        r, lw, k, v, a, b, g, bonus = prep[:8]
        v_g = [v_g0, prep[-1], v_g2]
        if l == 0:
            v_first = prep[8]
        yb = _scan_call(r, lw, k, v, a, b, bonus, g, vec(rwkv_lnx_g[l]), vec(rwkv_lnx_b[l]), batch, seq)
        outs, lses = [], []
        for gi, (window, dilation) in enumerate(ATTN_GROUPS):
            o, s = _attn_call(q_g[gi], k_g[gi], v_g[gi], window, dilation)
            outs.append(o)
            lses.append(s)
        h1, comb_t, cnt = _out_call(conv, yb, outs, lses, gates, h, lp, w_router_t, rbias, seq)
        counts = cnt[:, :, 0].reshape(t // MOE_TM, -1, N_EXPERTS).sum(axis=1).astype(jnp.int32)
        h = _moe_call(h1, comb_t, counts, lp, MOE_TM)
    return h.reshape(batch, seq, d)
```

```python
import functools

import jax
import jax.numpy as jnp
import numpy as np
from jax import lax
from jax.experimental import pallas as pl
from jax.experimental.pallas import tpu as pltpu

F32 = jnp.float32
BF16 = jnp.bfloat16

D_MODEL = 1024
HEAD_DIM = 64
N_HEADS = D_MODEL // HEAD_DIM
LANES = 128
HEADS_PER_VREG = LANES // HEAD_DIM
N_PAIRS = D_MODEL // LANES
DECAY_LORA = 64
AAA_LORA = 64
GATE_LORA = 160
MV_LORA = 32
LORA_PAD = 384
MV_PAD = 128
GN_EPS = 1e-5 * HEAD_DIM
ATTN_GROUPS = ((128, 1), (512, 4), (2048, 16))
ATTN_HEADS = 8
ATTN_GW = ATTN_HEADS * HEAD_DIM
ATTN_W = len(ATTN_GROUPS) * ATTN_GW
ATTN_BLOCK = 128
ATTN_QB = 2
ROT_DIM = 16
ROPE_THETA = 500000.0
N_EXPERTS = 16
EXPERTS_PER_GROUP = 4
D_EXPERT = 512
DEPTH = 2
ALPHA = (2 * DEPTH) ** 0.25
LN_EPS = 1e-5
NEG_INF = -1e30
SCAN_CHUNK = 64
SCAN_SUB = 4
OUT_SPLIT = 2
PREP_TM = 512
MOE_TM = 1024
MOE_ROWS = 176

VMEM_LIMIT = 56 * 1024 * 1024


def _cparams(sem):
    return pltpu.CompilerParams(dimension_semantics=sem, vmem_limit_bytes=VMEM_LIMIT)


def _layer_norm(x, g, b):
    mu = jnp.mean(x, axis=-1, keepdims=True)
    xc = x - mu
    var = jnp.mean(xc * xc, axis=-1, keepdims=True)
    return xc * lax.rsqrt(var + LN_EPS) * g + b


def _sigmoid(x):
    return 1.0 / (1.0 + jnp.exp(-x))


def _dot(a, b):
    return jnp.dot(a, b, preferred_element_type=F32)


def _dot_nt(a, b):
    return lax.dot_general(a, b, (((1,), (1,)), ((), ())), preferred_element_type=F32)


def _dot_tn(a, b):
    return lax.dot_general(a, b, (((0,), (0,)), ((), ())), preferred_element_type=F32)


def _shift_rows(x, carry8, n):
    rolled = pltpu.roll(x, n, axis=0)
    cr = pltpu.roll(carry8, n, axis=0)
    row = lax.broadcasted_iota(jnp.int32, (8, x.shape[1]), 0)
    top = jnp.where(row < n, cr, rolled[:8])
    return jnp.concatenate([top, rolled[8:]], axis=0)


def _store_residue_major(dst, tg, dil, c_rows):
    tm = tg.shape[0]
    if dil == 1:
        dst[0] = tg.astype(dst.dtype)
        return
    nl = ATTN_GW // LANES
    for j in range(nl):
        c_rows[j] = tg[:, j * LANES:(j + 1) * LANES]
    for res in range(dil):
        for j in range(nl):
            dst[res, :, j * LANES:(j + 1) * LANES] = (
                c_rows[j, pl.ds(res, tm // dil, stride=dil), :].astype(dst.dtype))


def _proj_gate_kernel(x_ref, w_ref, gate_o):
    gate_o[...] = _sigmoid(_dot(x_ref[...].astype(BF16), w_ref[...])).astype(gate_o.dtype)


def _ln_proj_gate_kernel(x_ref, w_ref, g_ref, b_ref, gate_o, h_o):
    h = _layer_norm(x_ref[...], g_ref[...], b_ref[...])
    h_o[...] = h
    gate_o[...] = _sigmoid(_dot(h.astype(BF16), w_ref[...])).astype(gate_o.dtype)


def _proj_conv_kernel(tiles_per_seq, x_ref, w_ref, convw_ref, conv_o, v0_o, carry):
    i = pl.program_id(0)

    @pl.when(i % tiles_per_seq == 0)
    def _():
        carry[...] = jnp.zeros_like(carry)

    acc = _dot(x_ref[...].astype(BF16), w_ref[...])
    tm = acc.shape[0]
    d = D_MODEL
    ch = acc[:, d:2 * d] * acc[:, 2 * d:3 * d]
    cr = carry[...]
    conv = (convw_ref[0:1, :] * _shift_rows(ch, cr, 2)
            + convw_ref[1:2, :] * _shift_rows(ch, cr, 1)
            + convw_ref[2:3, :] * ch)
    conv_o[...] = (acc[:, :d] * conv).astype(conv_o.dtype)
    carry[...] = ch[tm - 8:, :]
    v0_o[0] = acc[:, 3 * d:].astype(v0_o.dtype)


def _proj_attn_kernel(dil_v, x_ref, w_ref, rc_ref, rm_ref, rp_ref, *rest):
    n_g = len(ATTN_GROUPS)
    q_o, k_o, v_o, c_rows = rest[:n_g], rest[n_g:2 * n_g], rest[2 * n_g], rest[2 * n_g + 1]
    xb = x_ref[...].astype(BF16)
    reps = ATTN_W // LANES
    rc = jnp.concatenate([rc_ref[...]] * reps, axis=1)
    rm = jnp.concatenate([rm_ref[...]] * reps, axis=1)
    rp = jnp.concatenate([rp_ref[...]] * reps, axis=1)
    half = ROT_DIM // 2

    def rotate_store(t, dsts, scale):
        t = (t * rc + pltpu.roll(t, ATTN_W - half, axis=1) * rm + pltpu.roll(t, half, axis=1) * rp) * scale
        for gi, (_, dil) in enumerate(ATTN_GROUPS):
            _store_residue_major(dsts[gi], t[:, gi * ATTN_GW:(gi + 1) * ATTN_GW], dil, c_rows)

    tq = _dot(xb, w_ref[:, 0:ATTN_W])
    tk = _dot(xb, w_ref[:, ATTN_W:2 * ATTN_W])
    rotate_store(tq, q_o, HEAD_DIM ** -0.5)
    tv = _dot(xb, w_ref[:, 2 * ATTN_W:])
    rotate_store(tk, k_o, 1.0)
    _store_residue_major(v_o, tv, dil_v, c_rows)


def _proj_call(kernel, name, h, w, extra, extra_specs, out_specs, out_shape, scratch, sem, tm):
    t, d = h.shape
    return pl.pallas_call(
        kernel,
        grid=(t // tm,),
        in_specs=[pl.BlockSpec((tm, d), lambda i: (i, 0)), pl.BlockSpec(w.shape, lambda i: (0, 0))] + extra_specs,
        out_specs=out_specs,
        out_shape=out_shape,
        scratch_shapes=scratch,
        compiler_params=_cparams((sem,)),
        name=name,
    )(h, w, *extra)


def _in_proj(h, lp, rot_tabs, batch, seq, ln=None, tm=512):
    t, d = h.shape
    tps = seq // tm
    row = lambda w: pl.BlockSpec((tm, w), lambda i: (i, 0))
    res_spec = lambda dil: pl.BlockSpec((None, dil, tm // dil, ATTN_GW), lambda i: (i // tps, 0, i % tps, 0))
    res_shape = lambda dil: jax.ShapeDtypeStruct((batch, dil, seq // dil, ATTN_GW), BF16)
    rows_scratch = pltpu.VMEM((ATTN_GW // LANES, tm, LANES), F32)
    dils = [dil for _, dil in ATTN_GROUPS]
    if ln is None:
        gates, = _proj_call(
            _proj_gate_kernel, "proj_gate", h, lp["w_pg"], [], [],
            [row(3 * d)], [jax.ShapeDtypeStruct((t, 3 * d), BF16)], [], "parallel", tm)
    else:
        vec = pl.BlockSpec((1, d), lambda i: (0, 0))
        gates, h = _proj_call(
            _ln_proj_gate_kernel, "ln_proj_gate", h, lp["w_pg"], [a.reshape(1, d) for a in ln], [vec, vec],
            [row(3 * d), row(d)],
            [jax.ShapeDtypeStruct((t, 3 * d), BF16), jax.ShapeDtypeStruct((t, d), F32)], [], "parallel", tm)
    conv, v0 = _proj_call(
        functools.partial(_proj_conv_kernel, tps), "proj_conv", h, lp["w_pc"], [lp["conv_w"]],
        [pl.BlockSpec(lp["conv_w"].shape, lambda i: (0, 0))],
        [row(d), res_spec(dils[0])],
        [jax.ShapeDtypeStruct((t, d), BF16), res_shape(dils[0])],
        [pltpu.VMEM((8, d), F32)], "arbitrary", tm)
    qkv = _proj_call(
        functools.partial(_proj_attn_kernel, dils[2]), "proj_attn", h, lp["w_pa"], list(rot_tabs),
        [row(LANES)] * 3,
        [res_spec(dl) for dl in dils] * 2 + [res_spec(dils[2])],
        [res_shape(dl) for dl in dils] * 2 + [res_shape(dils[2])],
        [rows_scratch], "parallel", tm)
    n_g = len(ATTN_GROUPS)
    q, k, v2 = qkv[:n_g], qkv[n_g:2 * n_g], qkv[2 * n_g]
    return h, gates, conv, q, k, v0, v2


def _prep_kernel(has_vres, tiles_per_seq, dil,
                 x_ref, w_ref, mu_r_ref, mu_l_ref, w0_ref, w2_ref, a0_ref, a2_ref, g2_ref,
                 kk_ref, ka_ref, rk_ref, seg_ref, *rest):
    if has_vres:
        vf_ref, v0_ref, v1_ref, v2_ref = rest[:4]
        rest = rest[4:]
    r_o, lw_o, k_o, v_o, a_o, b_o, g_o, bonus_o = rest[:8]
    rest = rest[8:]
    if not has_vres:
        vf_o = rest[0]
        rest = rest[1:]
    vattn_o, c_zr, c_zl, c_rows = rest
    i = pl.program_id(0)

    @pl.when(i % tiles_per_seq == 0)
    def _():
        c_zr[...] = jnp.zeros_like(c_zr)
        c_zl[...] = jnp.zeros_like(c_zl)

    tm = x_ref.shape[0]
    d = D_MODEL
    xb = x_ref[...].astype(BF16)
    acc = _dot(xb, w_ref[...])
    _store_residue_major(vattn_o, acc[:, 3 * d + LORA_PAD:], dil, c_rows)
    zr = acc[:, :3 * d]
    zl = acc[:, 3 * d:3 * d + LORA_PAD]
    zr_m = zr + (_shift_rows(zr, c_zr[...], 1) - zr) * mu_r_ref[...]
    zl_m = zl + (_shift_rows(zl, c_zl[...], 1) - zl) * mu_l_ref[...]
    c_zr[...] = zr[tm - 8:, :]
    c_zl[...] = zl[tm - 8:, :]
    r = zr_m[:, :d]
    k = zr_m[:, d:2 * d]
    v = zr_m[:, 2 * d:]
    zl_wa = zl_m[:, :LANES]
    zl_g = zl_m[:, LANES:]
    x = w0_ref[...] + _dot(jnp.tanh(zl_wa).astype(BF16), w2_ref[...])
    lw_o[...] = -float(np.exp(-0.5)) * _sigmoid(x)
    if has_vres:
        lo = _dot(xb, v1_ref[...])
        gate_v = _sigmoid(v0_ref[...] + _dot(lo.astype(BF16), v2_ref[...]))
        v = v + (vf_ref[...].astype(F32) - v) * gate_v
    else:
        vf_o[...] = v.astype(vf_o.dtype)
    a_gate = _sigmoid(a0_ref[...] + _dot(zl_wa.astype(BF16), a2_ref[...]))
    g_o[...] = _dot(_sigmoid(zl_g).astype(BF16), g2_ref[...]).astype(g_o.dtype)
    seg = seg_ref[...]

    def head_sums(x):
        xb = x.astype(BF16)
        return jnp.concatenate([_dot(xb[:, p * LANES:(p + 1) * LANES], seg) for p in range(N_PAIRS)], axis=1)

    kk = k * kk_ref[...]
    kk = kk * lax.rsqrt(jnp.maximum(head_sums(kk * kk), 1e-24))
    k2 = k * (1.0 + (a_gate - 1.0) * ka_ref[...])
    bonus = head_sums(r * k2 * rk_ref[...]) * v
    r_o[...] = r.astype(r_o.dtype)
    k_o[...] = k2.astype(k_o.dtype)
    v_o[...] = v.astype(v_o.dtype)
    a_o[...] = (-kk).astype(a_o.dtype)
    b_o[...] = (kk * a_gate).astype(b_o.dtype)
    bonus_o[...] = bonus.astype(bonus_o.dtype)


def _prep_call(h, lp, v_first, dil, batch, seq, tm=PREP_TM):
    t, d = h.shape
    has_vres = v_first is not None
    tps = seq // tm
    row = lambda w: pl.BlockSpec((tm, w), lambda i: (i, 0))
    full = lambda a: pl.BlockSpec(a.shape, lambda i: (0,) * a.ndim)
    consts = [lp["w_pr"], lp["mu_r"], lp["mu_l"], lp["w0"], lp["w2"], lp["a0"], lp["a2"], lp["g2"],
              lp["k_k"], lp["k_a"], lp["r_k"], lp["seg"]]
    args = [h, *consts]
    in_specs = [row(d)] + [full(a) for a in consts]
    if has_vres:
        extra = [v_first, lp["v0"], lp["v1"], lp["v2"]]
        args += extra
        in_specs += [row(d)] + [full(a) for a in extra[1:]]
    out_dt = [BF16, F32, BF16, BF16, BF16, BF16, BF16, BF16] + ([] if has_vres else [BF16])
    return pl.pallas_call(
        functools.partial(_prep_kernel, has_vres, tps, dil),
        grid=(t // tm,),
        in_specs=in_specs,
        out_specs=[row(d) for _ in out_dt]
                  + [pl.BlockSpec((None, dil, tm // dil, ATTN_GW), lambda i: (i // tps, 0, i % tps, 0))],
        out_shape=[jax.ShapeDtypeStruct((t, d), dt) for dt in out_dt]
                  + [jax.ShapeDtypeStruct((batch, dil, seq // dil, ATTN_GW), BF16)],
        scratch_shapes=[pltpu.VMEM((8, 3 * d), F32), pltpu.VMEM((8, LORA_PAD), F32),
                        pltpu.VMEM((ATTN_GW // LANES, tm, LANES), F32)],
        compiler_params=_cparams(("arbitrary",)),
        name="proj_rwkv",
    )(*args)


def _scan_kernel(r_ref, lw_ref, k_ref, v_ref, a_ref, b_ref, bonus_ref, g_ref, lng_ref, lnb_ref,
                 o_ref, h_ref):
    c = pl.program_id(1)

    @pl.when(c == 0)
    def _():
        h_ref[...] = jnp.zeros_like(h_ref)

    C = SCAN_CHUNK
    n2 = HEADS_PER_VREG * C
    ri = lax.broadcasted_iota(jnp.int32, (C, C), 0)
    ci = lax.broadcasted_iota(jnp.int32, (C, C), 1)
    tri = (ri >= ci).astype(BF16)
    lane = lax.broadcasted_iota(jnp.int32, (1, LANES), 1)
    head_masks = [((lane // HEAD_DIM) == hh).astype(F32) for hh in range(HEADS_PER_VREG)]
    r2 = lax.broadcasted_iota(jnp.int32, (n2, 2 * n2), 0)
    c2 = lax.broadcasted_iota(jnp.int32, (n2, 2 * n2), 1)
    same = (r2 // C) == ((c2 % n2) // C)
    strict = same & ((r2 % C) > (c2 % C))
    incl = same & ((r2 % C) >= (c2 % C))
    strict_k = strict & (c2 >= n2)
    wn = max(n2, LANES)
    rw = lax.broadcasted_iota(jnp.int32, (n2, wn), 0)
    cw = lax.broadcasted_iota(jnp.int32, (n2, wn), 1)
    strict_b = ((rw // C) == (cw // C)) & ((rw % C) > (cw % C)) & (cw < n2)
    eye = (rw == cw).astype(F32)
    zero_rows = jnp.zeros((n2, LANES), BF16)

    def pad_rows(x):
        return x if wn == n2 else jnp.concatenate([x, jnp.zeros((wn - n2, x.shape[1]), x.dtype)], axis=0)
    lr = lax.broadcasted_iota(jnp.int32, (LANES, LANES), 0)
    lc = lax.broadcasted_iota(jnp.int32, (LANES, LANES), 1)
    seg_mean = jnp.where((lr // HEAD_DIM) == (lc // HEAD_DIM), 1.0 / HEAD_DIM, 0.0).astype(BF16)

    def stack(x):
        return jnp.concatenate([x * m for m in head_masks], axis=0).astype(BF16)

    n_sub = r_ref.shape[0] // C
    pairs = range(N_PAIRS)
    sls = [slice(p * LANES, (p + 1) * LANES) for p in pairs]
    items = [(s, p) for s in range(n_sub) for p in pairs]
    cums = []
    for s in range(n_sub):
        lw_s = lw_ref[s * C:(s + 1) * C, :]
        lw_hi = lw_s.astype(BF16)
        rem = lw_s - lw_hi.astype(F32)
        lw_mid = rem.astype(BF16)
        lw_lo = (rem - lw_mid.astype(F32)).astype(BF16)
        cums.append(_dot(tri, lw_hi) + (_dot(tri, lw_mid) + _dot(tri, lw_lo)))
    lhs, rhs, bk, v2, dec = [], [], [], [], []
    for s, p in items:
        rows, sl = slice(s * C, (s + 1) * C), sls[p]
        lw = lw_ref[rows, sl]
        cum = cums[s][:, sl]
        tot = cum[C - 1:C, :]
        e_bwd = jnp.exp(-cum)
        e_rest = jnp.exp(tot - cum)
        r = r_ref[rows, sl].astype(F32)
        k = k_ref[rows, sl].astype(F32)
        a = a_ref[rows, sl].astype(F32)
        b = b_ref[rows, sl].astype(F32)
        lhs.append(jnp.concatenate([stack(a * jnp.exp(cum - lw)), stack(r * jnp.exp(cum))], axis=0))
        rhs.append(jnp.concatenate([stack(b * e_bwd), stack(k * e_bwd)], axis=0))
        bk.append(jnp.concatenate([stack(b * e_rest), stack(k * e_rest)], axis=0))
        v2.append(stack(v_ref[rows, sl].astype(F32)))
        dec.append(jnp.exp(tot))
    amat = [_dot_nt(l, rr) for l, rr in zip(lhs, rhs)]
    a_ab = [jnp.where(strict_b, m[:n2, :wn], 0.0) for m in amat]
    a_ak = [jnp.where(strict_k, m[:n2, :], 0.0).astype(BF16) for m in amat]
    a_r = [jnp.where(incl, m[n2:, :], 0.0).astype(BF16) for m in amat]
    tinv = [eye + m for m in a_ab]
    pb = [m.astype(BF16) for m in a_ab]
    for _ in range(int(np.log2(C)) - 1):
        pb = [_dot(q, pad_rows(q)).astype(BF16) for q in pb]
        tinv = [t + _dot(q, pad_rows(t.astype(BF16))) for q, t in zip(pb, tinv)]
    ht = [h_ref[p] for p in pairs]
    ys = []
    for s in range(n_sub):
        it = [s * N_PAIRS + p for p in pairs]
        ah = [_dot_nt(lhs[i], ht[p].astype(BF16)) for p, i in zip(pairs, it)]
        w = [ah[p][:n2] + _dot(a_ak[i], jnp.concatenate([zero_rows, v2[i]], axis=0)) for p, i in zip(pairs, it)]
        u = [_dot(tinv[i].astype(BF16), pad_rows(w[p].astype(BF16))) for p, i in zip(pairs, it)]
        uv = [jnp.concatenate([u[p].astype(BF16), v2[i]], axis=0) for p, i in zip(pairs, it)]
        y2 = [ah[p][n2:] + _dot(a_r[i], uv[p]) for p, i in zip(pairs, it)]
        ht = [ht[p] * dec[i] + _dot_tn(uv[p], bk[i]) for p, i in zip(pairs, it)]
        for m in y2:
            y = m[:C]
            for hh in range(1, HEADS_PER_VREG):
                y = y + m[hh * C:(hh + 1) * C]
            ys.append(y)
    for p in pairs:
        h_ref[p] = ht[p]
    y = jnp.concatenate(ys, axis=0)
    y_hi = y.astype(BF16)
    ym = _dot(y_hi, seg_mean) + _dot((y - y_hi.astype(F32)).astype(BF16), seg_mean)
    yc = y - ym
    yv = _dot((yc * yc).astype(BF16), seg_mean)
    yn = yc * lax.rsqrt(yv + GN_EPS)
    for i, (s, p) in enumerate(items):
        rows, sl = slice(s * C, (s + 1) * C), sls[p]
        out = yn[i * C:(i + 1) * C] * lng_ref[:, sl] + lnb_ref[:, sl]
        out = (out + bonus_ref[rows, sl].astype(F32)) * g_ref[rows, sl].astype(F32)
        o_ref[rows, sl] = out.astype(o_ref.dtype)


def _scan_call(r, lw, k, v, a, b, bonus, g, lng, lnb, batch, seq):
    t, d = r.shape
    rows = SCAN_SUB * SCAN_CHUNK
    nc = seq // rows
    blk = pl.BlockSpec((rows, d), lambda bi, ci: (bi * nc + ci, 0))
    full = pl.BlockSpec((1, d), lambda bi, ci: (0, 0))
    return pl.pallas_call(
        _scan_kernel,
        grid=(batch, nc),
        in_specs=[blk] * 8 + [full, full],
        out_specs=blk,
        out_shape=jax.ShapeDtypeStruct((t, d), BF16),
        scratch_shapes=[pltpu.VMEM((N_PAIRS, LANES, LANES), F32)],
        compiler_params=_cparams(("parallel", "arbitrary")),
        name="wkv7_scan",
    )(r, lw, k, v, a, b, bonus, g, lng, lnb)


def _attn_kernel(span, q_ref, kp_ref, kc_ref, vp_ref, vc_ref, o_ref, lse_ref):
    n = pl.program_id(2)
    Q = ATTN_BLOCK
    qi = lax.broadcasted_iota(jnp.int32, (Q, 2 * Q), 0)
    kj = lax.broadcasted_iota(jnp.int32, (Q, 2 * Q), 1)
    dist = Q + qi - kj
    band = (dist >= 0) & (dist <= span)
    first = band & ((n > 0) | (kj >= Q))
    lane = lax.broadcasted_iota(jnp.int32, (1, LANES), 1)
    masks = [(lane // HEAD_DIM) == hh for hh in range(HEADS_PER_VREG)]
    sls = [slice(p * LANES, (p + 1) * LANES) for p in range(ATTN_GW // LANES)]
    qbs = range(ATTN_QB)

    def keys(prev_ref, cur_ref, j, sl):
        lo = prev_ref[:, sl] if j == 0 else cur_ref[(j - 1) * Q:j * Q, sl]
        return jnp.concatenate([lo, cur_ref[j * Q:(j + 1) * Q, sl]], axis=0)

    chains = [(j, sl, m) for j in qbs for sl in sls for m in masks]
    scores = []
    for j, sl, m in chains:
        q = q_ref[j * Q:(j + 1) * Q, sl]
        scores.append(_dot_nt(jnp.where(m, q, jnp.zeros_like(q)), keys(kp_ref, kc_ref, j, sl)))
    probs, maxes = [], []
    for (j, _, _), s in zip(chains, scores):
        s = jnp.where(first if j == 0 else band, s, NEG_INF)
        mx = jnp.max(s, axis=-1, keepdims=True)
        probs.append(jnp.exp((s - mx).astype(BF16)))
        maxes.append(mx)
    nh = len(masks)
    one = jnp.ones((), BF16)
    for c0 in range(0, len(chains), nh):
        j, sl, _ = chains[c0]
        vv = keys(vp_ref, vc_ref, j, sl)
        out = None
        lse = None
        for hh, m in enumerate(masks):
            ov = _dot(probs[c0 + hh], jnp.where(m, vv, one))
            den = pltpu.roll(ov, HEAD_DIM, axis=1)
            o = jnp.where(m, ov / den, 0.0)
            l = maxes[c0 + hh] + jnp.log(den)
            out = o if out is None else out + o
            lse = l if lse is None else jnp.where(m, l, lse)
        o_ref[j * Q:(j + 1) * Q, sl] = out.astype(o_ref.dtype)
        lse_ref[j * Q:(j + 1) * Q, sl] = lse


def _attn_call(q, k, v, window, dilation):
    batch, _, L, _ = q.shape
    Q = ATTN_BLOCK
    nb = L // (ATTN_QB * Q)
    cur = pl.BlockSpec((None, None, ATTN_QB * Q, ATTN_GW), lambda b, r, n: (b, r, n, 0))
    prev = pl.BlockSpec((None, None, Q, ATTN_GW),
                        lambda b, r, n: (b, r, jnp.maximum(ATTN_QB * n - 1, 0), 0))
    return pl.pallas_call(
        functools.partial(_attn_kernel, window // dilation),
        grid=(batch, dilation, nb),
        in_specs=[cur, prev, cur, prev, cur],
        out_specs=[cur, cur],
        out_shape=[jax.ShapeDtypeStruct(q.shape, BF16), jax.ShapeDtypeStruct(q.shape, F32)],
        compiler_params=_cparams(("parallel", "parallel", "arbitrary")),
        name=f"dilated_attn_d{dilation}",
    )(q, k, k, v, v)


def _route(logits, bias):
    mxl = jnp.max(logits, axis=0, keepdims=True)
    ex = jnp.exp(logits - mxl)
    probs = ex / jnp.sum(ex, axis=0, keepdims=True)
    sel = probs + bias
    rows = [sel[e:e + 1, :] for e in range(N_EXPERTS)]
    n_groups = N_EXPERTS // EXPERTS_PER_GROUP
    scores = []
    for gi in range(n_groups):
        a, b, c, dd = rows[EXPERTS_PER_GROUP * gi:EXPERTS_PER_GROUP * (gi + 1)]
        hi1, lo1 = jnp.maximum(a, b), jnp.minimum(a, b)
        hi2, lo2 = jnp.maximum(c, dd), jnp.minimum(c, dd)
        scores.append(jnp.maximum(hi1, hi2) + jnp.maximum(jnp.minimum(hi1, hi2), jnp.maximum(lo1, lo2)))
    best_score = scores[0]
    best = jnp.zeros_like(scores[0], dtype=jnp.int32)
    for gi in range(1, n_groups):
        better = scores[gi] > best_score
        best = jnp.where(better, gi, best)
        best_score = jnp.where(better, scores[gi], best_score)
    eid = lax.broadcasted_iota(jnp.int32, sel.shape, 0)
    masked = jnp.where((eid // EXPERTS_PER_GROUP) == best, sel, -jnp.inf)
    m1 = jnp.max(masked, axis=0, keepdims=True)
    i1 = jnp.min(jnp.where(masked == m1, eid, N_EXPERTS), axis=0, keepdims=True)
    masked2 = jnp.where(eid == i1, -jnp.inf, masked)
    m2 = jnp.max(masked2, axis=0, keepdims=True)
    i2 = jnp.min(jnp.where(masked2 == m2, eid, N_EXPERTS), axis=0, keepdims=True)
    p1 = jnp.sum(jnp.where(eid == i1, probs, 0.0), axis=0, keepdims=True)
    p2 = jnp.sum(jnp.where(eid == i2, probs, 0.0), axis=0, keepdims=True)
    tot = p1 + p2
    return jnp.where(eid == i1, p1 / tot, 0.0) + jnp.where(eid == i2, p2 / tot, 0.0)


def _out_kernel(conv_ref, yb_ref, o0_ref, o1_ref, o2_ref, l0_ref, l1_ref, l2_ref, zg_ref, h_ref,
                pa_ref, pb_ref, pc_ref, wo_ref, g_ref, b_ref, wr_ref, rb_ref, h1_ref, comb_ref, cnt_ref, sc_ref):
    d = D_MODEL
    tm = h_ref.shape[0]
    hm = tm // OUT_SPLIT
    nl = ATTN_GW // LANES
    dils = [dil for _, dil in ATTN_GROUPS]
    slot = iter(range(sc_ref.shape[0]))

    def token_order(ref, dil, s):
        n = hm // dil
        if dil == 1:
            return ref[0, s * hm:(s + 1) * hm, :].astype(F32)
        buf = next(slot)
        for res in range(dil):
            blk = ref[res, s * n:(s + 1) * n, :].astype(F32)
            for j in range(nl):
                sc_ref[buf, j, pl.ds(res, n, stride=dil), :] = blk[:, j * LANES:(j + 1) * LANES]
        return jnp.concatenate([sc_ref[buf, j] for j in range(nl)], axis=1)

    subs = range(OUT_SPLIT)
    rows = [slice(s * hm, (s + 1) * hm) for s in subs]
    ya, yb, ycs = [], [], []
    for s, r in enumerate(rows):
        ya.append(_dot(conv_ref[r, :], pa_ref[...]))
        l0, l1, l2 = [token_order(rf, dl, s) for rf, dl in zip((l0_ref, l1_ref, l2_ref), dils)]
        mx = jnp.maximum(jnp.maximum(l0, l1), l2)
        e0, e1, e2 = jnp.exp(l0 - mx), jnp.exp(l1 - mx), jnp.exp(l2 - mx)
        yb.append(_dot(yb_ref[r, :], pb_ref[...]))
        yc = e0 * token_order(o0_ref, dils[0], s)
        yc = yc + e1 * token_order(o1_ref, dils[1], s)
        yc = (yc + e2 * token_order(o2_ref, dils[2], s)) / (e0 + e1 + e2)
        ycs.append(yc.astype(BF16))
    yc_full = _dot(jnp.concatenate(ycs, axis=0), pc_ref[...])
    yc = [yc_full[r, :] for r in rows]
    mixes = []
    for s, r in enumerate(rows):
        ga = zg_ref[r, 0:d].astype(F32)
        gb = zg_ref[r, d:2 * d].astype(F32)
        gc = zg_ref[r, 2 * d:3 * d].astype(F32)
        mixes.append((ga * ya[s] + gb * yb[s] + gc * yc[s]).astype(BF16))
    outs = [_dot(m, wo_ref[...]) for m in mixes]
    wr = wr_ref[...]
    wr_hi = wr.astype(BF16)
    wr_hl = jnp.concatenate([wr_hi, (wr - wr_hi.astype(F32)).astype(BF16)], axis=0)
    logits = []
    for s, r in enumerate(rows):
        h1 = _layer_norm(ALPHA * h_ref[r, :] + outs[s], g_ref[...], b_ref[...])
        h1_ref[r, :] = h1
        h_hi = h1.astype(BF16)
        t1 = _dot_nt(wr_hl, h_hi)
        t2 = _dot_nt(wr_hi, (h1 - h_hi.astype(F32)).astype(BF16))
        logits.append(t1[:N_EXPERTS] + (t1[N_EXPERTS:] + t2))
    logits = jnp.concatenate(logits, axis=1)
    comb = _route(logits, rb_ref[...])
    comb_ref[...] = comb
    count = jnp.sum((comb > 0.0).astype(F32), axis=1, keepdims=True)
    cnt_ref[...] = jnp.broadcast_to(count, cnt_ref.shape)


def _out_call(conv, yb, outs, lses, z, h, lp, w_router_t, router_bias, seq, tm=512):
    t, d = h.shape
    tps = seq // tm
    row = lambda w, j=0: pl.BlockSpec((tm, w), lambda i, j=j: (i, j))
    full = lambda a: pl.BlockSpec(a.shape, lambda i: (0,) * a.ndim)
    res = [pl.BlockSpec((None, dil, tm // dil, ATTN_GW), lambda i: (i // tps, 0, i % tps, 0))
           for _, dil in ATTN_GROUPS]
    consts = [lp["p_a"], lp["p_b"], lp["p_c"], lp["w_o"], lp["ln1_g"], lp["ln1_b"], w_router_t, router_bias]
    n_relayout = 2 * OUT_SPLIT * sum(1 for _, dil in ATTN_GROUPS if dil > 1)
    return pl.pallas_call(
        _out_kernel,
        grid=(t // tm,),
        in_specs=[row(d), row(d)] + res + res + [row(3 * d), row(d)]
                 + [full(a) for a in consts],
        out_specs=[row(d), pl.BlockSpec((N_EXPERTS, tm), lambda i: (0, i)),
                   pl.BlockSpec((None, N_EXPERTS, LANES), lambda i: (i, 0, 0))],
        out_shape=[jax.ShapeDtypeStruct((t, d), F32), jax.ShapeDtypeStruct((N_EXPERTS, t), F32),
                   jax.ShapeDtypeStruct((t // tm, N_EXPERTS, LANES), F32)],
        scratch_shapes=[pltpu.VMEM((n_relayout, ATTN_GW // LANES, tm // OUT_SPLIT, LANES), F32)],
        compiler_params=_cparams(("parallel",)),
        name="mixer_out",
    )(conv, yb, *outs, *lses, z, h, *consts)


def _moe_kernel(cnt_ref, x_ref, comb_ref, wg_ref, wu_ref, wd_ref, g_ref, b_ref, o_ref,
                xb_ref, acc_ref, rank_ref):
    i = pl.program_id(0)
    e = pl.program_id(1)
    tm = x_ref.shape[0]
    R = MOE_ROWS

    @pl.when(e == 0)
    def _():
        xb_ref[...] = x_ref[...].astype(BF16)
        acc_ref[...] = jnp.zeros_like(acc_ref)
        sel = (comb_ref[...] > 0.0).astype(BF16)
        tri = (lax.broadcasted_iota(jnp.int32, (tm, tm), 0)
               <= lax.broadcasted_iota(jnp.int32, (tm, tm), 1)).astype(BF16)
        rank_ref[...] = _dot(sel, tri) - 1.0

    comb_e = comb_ref[pl.ds(e, 1), :]
    rank_e = jnp.where(comb_e > 0.0, rank_ref[pl.ds(e, 1), :], -1.0)
    slot = lax.broadcasted_iota(jnp.int32, (R, tm), 0).astype(F32)

    def block(blk, carry):
        hit = rank_e == slot + (blk * R).astype(F32)
        onehot = hit.astype(BF16)
        xg = _dot(onehot, xb_ref[...]).astype(BF16)
        cw = jnp.sum(jnp.where(hit, comb_e, 0.0), axis=1, keepdims=True)
        gate = _dot(xg, wg_ref[...])
        up = _dot(xg, wu_ref[...])
        hid = gate * _sigmoid(gate) * up * cw
        y = _dot(hid.astype(BF16), wd_ref[...]).astype(BF16)
        acc_ref[...] += _dot_tn(onehot, y)
        return carry

    lax.fori_loop(0, (cnt_ref[i, e] + R - 1) // R, block, 0)

    @pl.when(e == N_EXPERTS - 1)
    def _():
        o_ref[...] = _layer_norm(ALPHA * x_ref[...] + acc_ref[...], g_ref[...], b_ref[...])


def _moe_call(h1, comb_t, counts, lp, tm=1024):
    t, d = h1.shape
    wsp = lambda s: pl.BlockSpec((None,) + s, lambda i, e, c: (e, 0, 0))
    vec = pl.BlockSpec((1, d), lambda i, e, c: (0, 0))
    return pl.pallas_call(
        _moe_kernel,
        grid_spec=pltpu.PrefetchScalarGridSpec(
            num_scalar_prefetch=1,
            grid=(t // tm, N_EXPERTS),
            in_specs=[pl.BlockSpec((tm, d), lambda i, e, c: (i, 0)),
                      pl.BlockSpec((N_EXPERTS, tm), lambda i, e, c: (0, i)),
                      wsp((d, D_EXPERT)), wsp((d, D_EXPERT)), wsp((D_EXPERT, d)), vec, vec],
            out_specs=pl.BlockSpec((tm, d), lambda i, e, c: (i, 0)),
            scratch_shapes=[pltpu.VMEM((tm, d), BF16), pltpu.VMEM((tm, d), F32),
                            pltpu.VMEM((N_EXPERTS, tm), F32)]),
        out_shape=jax.ShapeDtypeStruct((t, d), F32),
        compiler_params=_cparams(("parallel", "arbitrary")),
        name="moe",
    )(counts, h1, comb_t, lp["w_gate"], lp["w_up"], lp["w_down"], lp["ln2_g"], lp["ln2_b"])


def _pad_rows(w, start, total):
    return jnp.zeros((total, w.shape[1]), w.dtype).at[start:start + w.shape[0]].set(w)


def _rotary_tables(positions):
    half = ROT_DIM // 2
    inv_freq = ROPE_THETA ** (-jnp.arange(0, ROT_DIM, 2, dtype=F32) / ROT_DIM)
    ang = positions.astype(F32).reshape(-1, 1) * inv_freq
    cos, sin = jnp.cos(ang), jnp.sin(ang)
    t = ang.shape[0]
    pad = HEAD_DIM - ROT_DIM
    rc = jnp.concatenate([cos, cos, jnp.ones((t, pad), F32)], axis=1)
    rm = jnp.concatenate([-sin, jnp.zeros((t, half + pad), F32)], axis=1)
    rp = jnp.concatenate([jnp.zeros((t, half), F32), sin, jnp.zeros((t, pad), F32)], axis=1)
    return tuple(jnp.tile(x, (1, HEADS_PER_VREG)) for x in (rc, rm, rp))


def _pack_in_proj(w):
    d = D_MODEL
    n_lora = DECAY_LORA + AAA_LORA + GATE_LORA
    gate, conv, rkv = w[:, :3 * d], w[:, 3 * d:6 * d], w[:, 6 * d:9 * d]
    lora = jnp.pad(w[:, 9 * d:9 * d + n_lora], ((0, 0), (0, LORA_PAD - n_lora)))
    attn = w[:, 9 * d + n_lora:]
    q, k, v = attn[:, :ATTN_W], attn[:, ATTN_W:2 * ATTN_W], attn[:, 2 * ATTN_W:]
    vg = [v[:, gi * ATTN_GW:(gi + 1) * ATTN_GW] for gi in range(len(ATTN_GROUPS))]
    cat = lambda parts: jnp.concatenate(parts, axis=1).astype(BF16)
    return {"w_pg": cat([gate]), "w_pc": cat([conv, vg[0]]), "w_pr": cat([rkv, lora, vg[1]]),
            "w_pa": cat([q, k, vg[2]])}


def kernel(x, positions, ln_in_g, ln_in_b, w_in, conv_w, rwkv_mu, rwkv_w0, rwkv_w2, rwkv_a0, rwkv_a2, rwkv_g2, rwkv_v0, rwkv_v1, rwkv_v2, rwkv_k_k, rwkv_k_a, rwkv_r_k, rwkv_lnx_g, rwkv_lnx_b, p_a, p_b, p_c, w_o, ln1_g, ln1_b, w_router, router_bias, w_gate, w_up, w_down, ln2_g, ln2_b):
    batch, seq, d = x.shape
    t = batch * seq
    depth = w_in.shape[0]
    assert d == D_MODEL and seq % (ATTN_QB * ATTN_BLOCK * ATTN_GROUPS[-1][1]) == 0
    rot_tabs = _rotary_tables(positions)
    hid = lax.broadcasted_iota(jnp.int32, (LANES, LANES), 0) // HEAD_DIM
    seg = (hid == hid.T).astype(BF16)
    w_router_t = w_router.T.astype(F32)
    rbias = router_bias.reshape(N_EXPERTS, 1).astype(F32)
    vec = lambda a: a.reshape(1, -1).astype(F32)

    h = x.reshape(t, d)
    v_first = None
    for l in range(depth):
        lp = {
            "conv_w": conv_w[l],
            "mu_r": vec(rwkv_mu[l][:3 * d]),
            "mu_l": vec(jnp.pad(rwkv_mu[l][3 * d:], (0, LORA_PAD - 288))),
            "w0": vec(rwkv_w0[l]),
            "w2": _pad_rows(rwkv_w2[l], 0, LANES).astype(BF16),
            "a0": vec(rwkv_a0[l]),
            "a2": _pad_rows(rwkv_a2[l], DECAY_LORA, LANES).astype(BF16),
            "g2": _pad_rows(rwkv_g2[l], 0, LORA_PAD - LANES).astype(BF16),
            "k_k": vec(rwkv_k_k[l]), "k_a": vec(rwkv_k_a[l]), "r_k": vec(rwkv_r_k[l]),
            "seg": seg,
            "p_a": p_a[l].astype(BF16), "p_b": p_b[l].astype(BF16), "p_c": p_c[l].astype(BF16),
            "w_o": w_o[l].astype(BF16), "ln1_g": vec(ln1_g[l]), "ln1_b": vec(ln1_b[l]),
            "w_gate": w_gate[l].astype(BF16), "w_up": w_up[l].astype(BF16), "w_down": w_down[l].astype(BF16),
            "ln2_g": vec(ln2_g[l]), "ln2_b": vec(ln2_b[l]),
        }
        if l > 0:
            lp["v0"] = vec(rwkv_v0[l - 1])
            lp["v1"] = jnp.pad(rwkv_v1[l - 1], ((0, 0), (0, MV_PAD - MV_LORA))).astype(BF16)
            lp["v2"] = _pad_rows(rwkv_v2[l - 1], 0, MV_PAD).astype(BF16)
        lp.update(_pack_in_proj(w_in[l]))
        h, gates, conv, q_g, k_g, v_g0, v_g2 = _in_proj(
            h, lp, rot_tabs, batch, seq, ln=(ln_in_g, ln_in_b) if l == 0 else None)
        prep = _prep_call(h, lp, v_first, ATTN_GROUPS[1][1], batch, seq)
        r, lw, k, v, a, b, g, bonus = prep[:8]
        v_g = [v_g0, prep[-1], v_g2]
        if l == 0:
            v_first = prep[8]
        yb = _scan_call(r, lw, k, v, a, b, bonus, g, vec(rwkv_lnx_g[l]), vec(rwkv_lnx_b[l]), batch, seq)
        outs, lses = [], []
        for gi, (window, dilation) in enumerate(ATTN_GROUPS):
            o, s = _attn_call(q_g[gi], k_g[gi], v_g[gi], window, dilation)
            outs.append(o)
            lses.append(s)
        h1, comb_t, cnt = _out_call(conv, yb, outs, lses, gates, h, lp, w_router_t, rbias, seq)
        counts = cnt[:, :, 0].reshape(t // MOE_TM, -1, N_EXPERTS).sum(axis=1).astype(jnp.int32)
        h = _moe_call(h1, comb_t, counts, lp, MOE_TM)
    return h.reshape(batch, seq, d)
```

```python
import functools

import jax
import jax.numpy as jnp
import numpy as np
from jax import lax
from jax.experimental import pallas as pl
from jax.experimental.pallas import tpu as pltpu

F32 = jnp.float32
BF16 = jnp.bfloat16

D_MODEL = 1024
HEAD_DIM = 64
N_HEADS = D_MODEL // HEAD_DIM
LANES = 128
HEADS_PER_VREG = LANES // HEAD_DIM
N_PAIRS = D_MODEL // LANES
DECAY_LORA = 64
AAA_LORA = 64
GATE_LORA = 160
MV_LORA = 32
LORA_PAD = 384
MV_PAD = 128
GN_EPS = 1e-5 * HEAD_DIM
ATTN_GROUPS = ((128, 1), (512, 4), (2048, 16))
ATTN_HEADS = 8
ATTN_GW = ATTN_HEADS * HEAD_DIM
ATTN_W = len(ATTN_GROUPS) * ATTN_GW
ATTN_BLOCK = 128
ATTN_QB = 4
ROT_DIM = 16
ROPE_THETA = 500000.0
N_EXPERTS = 16
EXPERTS_PER_GROUP = 4
D_EXPERT = 512
DEPTH = 2
ALPHA = (2 * DEPTH) ** 0.25
LN_EPS = 1e-5
NEG_INF = -1e30
SCAN_CHUNK = 64
SCAN_SUB = 8
OUT_SPLIT = 2
PREP_TM = 512
MOE_TM = 1024
MOE_ROWS = 176

VMEM_LIMIT = 56 * 1024 * 1024


def _cparams(sem):
    return pltpu.CompilerParams(dimension_semantics=sem, vmem_limit_bytes=VMEM_LIMIT)


def _layer_norm(x, g, b):
    mu = jnp.mean(x, axis=-1, keepdims=True)
    xc = x - mu
    var = jnp.mean(xc * xc, axis=-1, keepdims=True)
    return xc * lax.rsqrt(var + LN_EPS) * g + b


def _sigmoid(x):
    return 1.0 / (1.0 + jnp.exp(-x))


def _dot(a, b):
    return jnp.dot(a, b, preferred_element_type=F32)


def _dot_nt(a, b):
    return lax.dot_general(a, b, (((1,), (1,)), ((), ())), preferred_element_type=F32)


def _dot_tn(a, b):
    return lax.dot_general(a, b, (((0,), (0,)), ((), ())), preferred_element_type=F32)


def _shift_rows(x, carry8, n):
    rolled = pltpu.roll(x, n, axis=0)
    cr = pltpu.roll(carry8, n, axis=0)
    row = lax.broadcasted_iota(jnp.int32, (8, x.shape[1]), 0)
    top = jnp.where(row < n, cr, rolled[:8])
    return jnp.concatenate([top, rolled[8:]], axis=0)


def _store_residue_major(dst, tg, dil, c_rows):
    tm = tg.shape[0]
    if dil == 1:
        dst[0] = tg.astype(dst.dtype)
        return
    nl = ATTN_GW // LANES
    for j in range(nl):
        c_rows[j] = tg[:, j * LANES:(j + 1) * LANES]
    for res in range(dil):
        for j in range(nl):
            dst[res, :, j * LANES:(j + 1) * LANES] = (
                c_rows[j, pl.ds(res, tm // dil, stride=dil), :].astype(dst.dtype))


def _proj_gate_kernel(x_ref, w_ref, gate_o):
    gate_o[...] = _sigmoid(_dot(x_ref[...].astype(BF16), w_ref[...])).astype(gate_o.dtype)


def _ln_proj_gate_kernel(x_ref, w_ref, g_ref, b_ref, gate_o, h_o):
    h = _layer_norm(x_ref[...], g_ref[...], b_ref[...])
    h_o[...] = h
    gate_o[...] = _sigmoid(_dot(h.astype(BF16), w_ref[...])).astype(gate_o.dtype)


def _proj_conv_kernel(tiles_per_seq, x_ref, w_ref, convw_ref, conv_o, v0_o, carry):
    i = pl.program_id(0)

    @pl.when(i % tiles_per_seq == 0)
    def _():
        carry[...] = jnp.zeros_like(carry)

    acc = _dot(x_ref[...].astype(BF16), w_ref[...])
    tm = acc.shape[0]
    d = D_MODEL
    ch = acc[:, d:2 * d] * acc[:, 2 * d:3 * d]
    cr = carry[...]
    conv = (convw_ref[0:1, :] * _shift_rows(ch, cr, 2)
            + convw_ref[1:2, :] * _shift_rows(ch, cr, 1)
            + convw_ref[2:3, :] * ch)
    conv_o[...] = (acc[:, :d] * conv).astype(conv_o.dtype)
    carry[...] = ch[tm - 8:, :]
    v0_o[0] = acc[:, 3 * d:].astype(v0_o.dtype)


def _proj_attn_kernel(dil_v, x_ref, w_ref, rc_ref, rm_ref, rp_ref, *rest):
    n_g = len(ATTN_GROUPS)
    q_o, k_o, v_o, c_rows = rest[:n_g], rest[n_g:2 * n_g], rest[2 * n_g], rest[2 * n_g + 1]
    xb = x_ref[...].astype(BF16)
    reps = ATTN_W // LANES
    rc = jnp.concatenate([rc_ref[...]] * reps, axis=1)
    rm = jnp.concatenate([rm_ref[...]] * reps, axis=1)
    rp = jnp.concatenate([rp_ref[...]] * reps, axis=1)
    half = ROT_DIM // 2

    def rotate_store(t, dsts, scale):
        t = (t * rc + pltpu.roll(t, ATTN_W - half, axis=1) * rm + pltpu.roll(t, half, axis=1) * rp) * scale
        for gi, (_, dil) in enumerate(ATTN_GROUPS):
            _store_residue_major(dsts[gi], t[:, gi * ATTN_GW:(gi + 1) * ATTN_GW], dil, c_rows)

    tq = _dot(xb, w_ref[:, 0:ATTN_W])
    tk = _dot(xb, w_ref[:, ATTN_W:2 * ATTN_W])
    rotate_store(tq, q_o, HEAD_DIM ** -0.5)
    tv = _dot(xb, w_ref[:, 2 * ATTN_W:])
    rotate_store(tk, k_o, 1.0)
    _store_residue_major(v_o, tv, dil_v, c_rows)


def _proj_call(kernel, name, h, w, extra, extra_specs, out_specs, out_shape, scratch, sem, tm):
    t, d = h.shape
    return pl.pallas_call(
        kernel,
        grid=(t // tm,),
        in_specs=[pl.BlockSpec((tm, d), lambda i: (i, 0)), pl.BlockSpec(w.shape, lambda i: (0, 0))] + extra_specs,
        out_specs=out_specs,
        out_shape=out_shape,
        scratch_shapes=scratch,
        compiler_params=_cparams((sem,)),
        name=name,
    )(h, w, *extra)


def _in_proj(h, lp, rot_tabs, batch, seq, ln=None, tm=512):
    t, d = h.shape
    tps = seq // tm
    row = lambda w: pl.BlockSpec((tm, w), lambda i: (i, 0))
    res_spec = lambda dil: pl.BlockSpec((None, dil, tm // dil, ATTN_GW), lambda i: (i // tps, 0, i % tps, 0))
    res_shape = lambda dil: jax.ShapeDtypeStruct((batch, dil, seq // dil, ATTN_GW), BF16)
    rows_scratch = pltpu.VMEM((ATTN_GW // LANES, tm, LANES), F32)
    dils = [dil for _, dil in ATTN_GROUPS]
    if ln is None:
        gates, = _proj_call(
            _proj_gate_kernel, "proj_gate", h, lp["w_pg"], [], [],
            [row(3 * d)], [jax.ShapeDtypeStruct((t, 3 * d), BF16)], [], "parallel", tm)
    else:
        vec = pl.BlockSpec((1, d), lambda i: (0, 0))
        gates, h = _proj_call(
            _ln_proj_gate_kernel, "ln_proj_gate", h, lp["w_pg"], [a.reshape(1, d) for a in ln], [vec, vec],
            [row(3 * d), row(d)],
            [jax.ShapeDtypeStruct((t, 3 * d), BF16), jax.ShapeDtypeStruct((t, d), F32)], [], "parallel", tm)
    conv, v0 = _proj_call(
        functools.partial(_proj_conv_kernel, tps), "proj_conv", h, lp["w_pc"], [lp["conv_w"]],
        [pl.BlockSpec(lp["conv_w"].shape, lambda i: (0, 0))],
        [row(d), res_spec(dils[0])],
        [jax.ShapeDtypeStruct((t, d), BF16), res_shape(dils[0])],
        [pltpu.VMEM((8, d), F32)], "arbitrary", tm)
    qkv = _proj_call(
        functools.partial(_proj_attn_kernel, dils[2]), "proj_attn", h, lp["w_pa"], list(rot_tabs),
        [row(LANES)] * 3,
        [res_spec(dl) for dl in dils] * 2 + [res_spec(dils[2])],
        [res_shape(dl) for dl in dils] * 2 + [res_shape(dils[2])],
        [rows_scratch], "parallel", tm)
    n_g = len(ATTN_GROUPS)
    q, k, v2 = qkv[:n_g], qkv[n_g:2 * n_g], qkv[2 * n_g]
    return h, gates, conv, q, k, v0, v2


def _prep_kernel(has_vres, tiles_per_seq, dil,
                 x_ref, w_ref, mu_r_ref, mu_l_ref, w0_ref, w2_ref, a0_ref, a2_ref, g2_ref,
                 kk_ref, ka_ref, rk_ref, seg_ref, *rest):
    if has_vres:
        vf_ref, v0_ref, v1_ref, v2_ref = rest[:4]
        rest = rest[4:]
    r_o, lw_o, k_o, v_o, a_o, b_o, g_o, bonus_o = rest[:8]
    rest = rest[8:]
    if not has_vres:
        vf_o = rest[0]
        rest = rest[1:]
    vattn_o, c_zr, c_zl, c_rows = rest
    i = pl.program_id(0)

    @pl.when(i % tiles_per_seq == 0)
    def _():
        c_zr[...] = jnp.zeros_like(c_zr)
        c_zl[...] = jnp.zeros_like(c_zl)

    tm = x_ref.shape[0]
    d = D_MODEL
    xb = x_ref[...].astype(BF16)
    acc = _dot(xb, w_ref[...])
    _store_residue_major(vattn_o, acc[:, 3 * d + LORA_PAD:], dil, c_rows)
    zr = acc[:, :3 * d]
    zl = acc[:, 3 * d:3 * d + LORA_PAD]
    zr_m = zr + (_shift_rows(zr, c_zr[...], 1) - zr) * mu_r_ref[...]
    zl_m = zl + (_shift_rows(zl, c_zl[...], 1) - zl) * mu_l_ref[...]
    c_zr[...] = zr[tm - 8:, :]
    c_zl[...] = zl[tm - 8:, :]
    r = zr_m[:, :d]
    k = zr_m[:, d:2 * d]
    v = zr_m[:, 2 * d:]
    zl_wa = zl_m[:, :LANES]
    zl_g = zl_m[:, LANES:]
    x = w0_ref[...] + _dot(jnp.tanh(zl_wa).astype(BF16), w2_ref[...])
    lw_o[...] = -float(np.exp(-0.5)) * _sigmoid(x)
    if has_vres:
        lo = _dot(xb, v1_ref[...])
        gate_v = _sigmoid(v0_ref[...] + _dot(lo.astype(BF16), v2_ref[...]))
        v = v + (vf_ref[...].astype(F32) - v) * gate_v
    else:
        vf_o[...] = v.astype(vf_o.dtype)
    a_gate = _sigmoid(a0_ref[...] + _dot(zl_wa.astype(BF16), a2_ref[...]))
    g_o[...] = _dot(_sigmoid(zl_g).astype(BF16), g2_ref[...]).astype(g_o.dtype)
    seg = seg_ref[...]

    def head_sums(x):
        xb = x.astype(BF16)
        return jnp.concatenate([_dot(xb[:, p * LANES:(p + 1) * LANES], seg) for p in range(N_PAIRS)], axis=1)

    kk = k * kk_ref[...]
    kk = kk * lax.rsqrt(jnp.maximum(head_sums(kk * kk), 1e-24))
    k2 = k * (1.0 + (a_gate - 1.0) * ka_ref[...])
    bonus = head_sums(r * k2 * rk_ref[...]) * v
    r_o[...] = r.astype(r_o.dtype)
    k_o[...] = k2.astype(k_o.dtype)
    v_o[...] = v.astype(v_o.dtype)
    a_o[...] = (-kk).astype(a_o.dtype)
    b_o[...] = (kk * a_gate).astype(b_o.dtype)
    bonus_o[...] = bonus.astype(bonus_o.dtype)


def _prep_call(h, lp, v_first, dil, batch, seq, tm=PREP_TM):
    t, d = h.shape
    has_vres = v_first is not None
    tps = seq // tm
    row = lambda w: pl.BlockSpec((tm, w), lambda i: (i, 0))
    full = lambda a: pl.BlockSpec(a.shape, lambda i: (0,) * a.ndim)
    consts = [lp["w_pr"], lp["mu_r"], lp["mu_l"], lp["w0"], lp["w2"], lp["a0"], lp["a2"], lp["g2"],
              lp["k_k"], lp["k_a"], lp["r_k"], lp["seg"]]
    args = [h, *consts]
    in_specs = [row(d)] + [full(a) for a in consts]
    if has_vres:
        extra = [v_first, lp["v0"], lp["v1"], lp["v2"]]
        args += extra
        in_specs += [row(d)] + [full(a) for a in extra[1:]]
    out_dt = [BF16, F32, BF16, BF16, BF16, BF16, BF16, BF16] + ([] if has_vres else [BF16])
    return pl.pallas_call(
        functools.partial(_prep_kernel, has_vres, tps, dil),
        grid=(t // tm,),
        in_specs=in_specs,
        out_specs=[row(d) for _ in out_dt]
                  + [pl.BlockSpec((None, dil, tm // dil, ATTN_GW), lambda i: (i // tps, 0, i % tps, 0))],
        out_shape=[jax.ShapeDtypeStruct((t, d), dt) for dt in out_dt]
                  + [jax.ShapeDtypeStruct((batch, dil, seq // dil, ATTN_GW), BF16)],
        scratch_shapes=[pltpu.VMEM((8, 3 * d), F32), pltpu.VMEM((8, LORA_PAD), F32),
                        pltpu.VMEM((ATTN_GW // LANES, tm, LANES), F32)],
        compiler_params=_cparams(("arbitrary",)),
        name="proj_rwkv",
    )(*args)


def _scan_kernel(r_ref, lw_ref, k_ref, v_ref, a_ref, b_ref, bonus_ref, g_ref, lng_ref, lnb_ref,
                 o_ref, h_ref):
    c = pl.program_id(1)

    @pl.when(c == 0)
    def _():
        h_ref[...] = jnp.zeros_like(h_ref)

    C = SCAN_CHUNK
    n2 = HEADS_PER_VREG * C
    ri = lax.broadcasted_iota(jnp.int32, (C, C), 0)
    ci = lax.broadcasted_iota(jnp.int32, (C, C), 1)
    tri = (ri >= ci).astype(BF16)
    lane = lax.broadcasted_iota(jnp.int32, (1, LANES), 1)
    head_masks = [((lane // HEAD_DIM) == hh).astype(F32) for hh in range(HEADS_PER_VREG)]
    r2 = lax.broadcasted_iota(jnp.int32, (n2, 2 * n2), 0)
    c2 = lax.broadcasted_iota(jnp.int32, (n2, 2 * n2), 1)
    same = (r2 // C) == ((c2 % n2) // C)
    strict = same & ((r2 % C) > (c2 % C))
    incl = same & ((r2 % C) >= (c2 % C))
    strict_k = strict & (c2 >= n2)
    wn = max(n2, LANES)
    rw = lax.broadcasted_iota(jnp.int32, (n2, wn), 0)
    cw = lax.broadcasted_iota(jnp.int32, (n2, wn), 1)
    strict_b = ((rw // C) == (cw // C)) & ((rw % C) > (cw % C)) & (cw < n2)
    eye = (rw == cw).astype(F32)
    zero_rows = jnp.zeros((n2, LANES), BF16)

    def pad_rows(x):
        return x if wn == n2 else jnp.concatenate([x, jnp.zeros((wn - n2, x.shape[1]), x.dtype)], axis=0)
    lr = lax.broadcasted_iota(jnp.int32, (LANES, LANES), 0)
    lc = lax.broadcasted_iota(jnp.int32, (LANES, LANES), 1)
    seg_mean = jnp.where((lr // HEAD_DIM) == (lc // HEAD_DIM), 1.0 / HEAD_DIM, 0.0).astype(BF16)

    def stack(x):
        return jnp.concatenate([x * m for m in head_masks], axis=0).astype(BF16)

    n_sub = r_ref.shape[0] // C
    pairs = range(N_PAIRS)
    sls = [slice(p * LANES, (p + 1) * LANES) for p in pairs]
    items = [(s, p) for s in range(n_sub) for p in pairs]
    cums = []
    for s in range(n_sub):
        lw_s = lw_ref[s * C:(s + 1) * C, :]
        lw_hi = lw_s.astype(BF16)
        rem = lw_s - lw_hi.astype(F32)
        lw_mid = rem.astype(BF16)
        lw_lo = (rem - lw_mid.astype(F32)).astype(BF16)
        cums.append(_dot(tri, lw_hi) + (_dot(tri, lw_mid) + _dot(tri, lw_lo)))
    lhs, rhs, bk, v2, dec = [], [], [], [], []
    for s, p in items:
        rows, sl = slice(s * C, (s + 1) * C), sls[p]
        lw = lw_ref[rows, sl]
        cum = cums[s][:, sl]
        tot = cum[C - 1:C, :]
        e_bwd = jnp.exp(-cum)
        e_rest = jnp.exp(tot - cum)
        r = r_ref[rows, sl].astype(F32)
        k = k_ref[rows, sl].astype(F32)
        a = a_ref[rows, sl].astype(F32)
        b = b_ref[rows, sl].astype(F32)
        lhs.append(jnp.concatenate([stack(a * jnp.exp(cum - lw)), stack(r * jnp.exp(cum))], axis=0))
        rhs.append(jnp.concatenate([stack(b * e_bwd), stack(k * e_bwd)], axis=0))
        bk.append(jnp.concatenate([stack(b * e_rest), stack(k * e_rest)], axis=0))
        v2.append(stack(v_ref[rows, sl].astype(F32)))
        dec.append(jnp.exp(tot))
    amat = [_dot_nt(l, rr) for l, rr in zip(lhs, rhs)]
    a_ab = [jnp.where(strict_b, m[:n2, :wn], 0.0) for m in amat]
    a_ak = [jnp.where(strict_k, m[:n2, :], 0.0).astype(BF16) for m in amat]
    a_r = [jnp.where(incl, m[n2:, :], 0.0).astype(BF16) for m in amat]
    tinv = [eye + m for m in a_ab]
    pb = [m.astype(BF16) for m in a_ab]
    for _ in range(int(np.log2(C)) - 1):
        pb = [_dot(q, pad_rows(q)).astype(BF16) for q in pb]
        tinv = [t + _dot(q, pad_rows(t.astype(BF16))) for q, t in zip(pb, tinv)]
    ht = [h_ref[p] for p in pairs]
    ys = []
    for s in range(n_sub):
        it = [s * N_PAIRS + p for p in pairs]
        ah = [_dot_nt(lhs[i], ht[p].astype(BF16)) for p, i in zip(pairs, it)]
        w = [ah[p][:n2] + _dot(a_ak[i], jnp.concatenate([zero_rows, v2[i]], axis=0)) for p, i in zip(pairs, it)]
        u = [_dot(tinv[i].astype(BF16), pad_rows(w[p].astype(BF16))) for p, i in zip(pairs, it)]
        uv = [jnp.concatenate([u[p].astype(BF16), v2[i]], axis=0) for p, i in zip(pairs, it)]
        y2 = [ah[p][n2:] + _dot(a_r[i], uv[p]) for p, i in zip(pairs, it)]
        ht = [ht[p] * dec[i] + _dot_tn(uv[p], bk[i]) for p, i in zip(pairs, it)]
        for m in y2:
            y = m[:C]
            for hh in range(1, HEADS_PER_VREG):
                y = y + m[hh * C:(hh + 1) * C]
            ys.append(y)
    for p in pairs:
        h_ref[p] = ht[p]
    y = jnp.concatenate(ys, axis=0)
    y_hi = y.astype(BF16)
    ym = _dot(y_hi, seg_mean) + _dot((y - y_hi.astype(F32)).astype(BF16), seg_mean)
    yc = y - ym
    yv = _dot((yc * yc).astype(BF16), seg_mean)
    yn = yc * lax.rsqrt(yv + GN_EPS)
    for i, (s, p) in enumerate(items):
        rows, sl = slice(s * C, (s + 1) * C), sls[p]
        out = yn[i * C:(i + 1) * C] * lng_ref[:, sl] + lnb_ref[:, sl]
        out = (out + bonus_ref[rows, sl].astype(F32)) * g_ref[rows, sl].astype(F32)
        o_ref[rows, sl] = out.astype(o_ref.dtype)


def _scan_call(r, lw, k, v, a, b, bonus, g, lng, lnb, batch, seq):
    t, d = r.shape
    rows = SCAN_SUB * SCAN_CHUNK
    nc = seq // rows
    blk = pl.BlockSpec((rows, d), lambda bi, ci: (bi * nc + ci, 0))
    full = pl.BlockSpec((1, d), lambda bi, ci: (0, 0))
    return pl.pallas_call(
        _scan_kernel,
        grid=(batch, nc),
        in_specs=[blk] * 8 + [full, full],
        out_specs=blk,
        out_shape=jax.ShapeDtypeStruct((t, d), BF16),
        scratch_shapes=[pltpu.VMEM((N_PAIRS, LANES, LANES), F32)],
        compiler_params=_cparams(("parallel", "arbitrary")),
        name="wkv7_scan",
    )(r, lw, k, v, a, b, bonus, g, lng, lnb)


def _attn_kernel(span, q_ref, kp_ref, kc_ref, vp_ref, vc_ref, o_ref, lse_ref):
    n = pl.program_id(2)
    Q = ATTN_BLOCK
    qbs = range(q_ref.shape[0] // Q)
    qi = lax.broadcasted_iota(jnp.int32, (Q, 2 * Q), 0)
    kj = lax.broadcasted_iota(jnp.int32, (Q, 2 * Q), 1)
    dist = Q + qi - kj
    band = (dist >= 0) & (dist <= span)
    first = band & ((n > 0) | (kj >= Q))
    lane = lax.broadcasted_iota(jnp.int32, (1, LANES), 1)
    masks = [(lane // HEAD_DIM) == hh for hh in range(HEADS_PER_VREG)]
    sls = [slice(p * LANES, (p + 1) * LANES) for p in range(ATTN_GW // LANES)]

    def keys(prev_ref, cur_ref, j, sl):
        lo = prev_ref[:, sl] if j == 0 else cur_ref[(j - 1) * Q:j * Q, sl]
        return jnp.concatenate([lo, cur_ref[j * Q:(j + 1) * Q, sl]], axis=0)

    chains = [(j, sl, m) for j in qbs for sl in sls for m in masks]
    scores = []
    for j, sl, m in chains:
        q = q_ref[j * Q:(j + 1) * Q, sl]
        scores.append(_dot_nt(jnp.where(m, q, jnp.zeros_like(q)), keys(kp_ref, kc_ref, j, sl)))
    probs, maxes = [], []
    for (j, _, _), s in zip(chains, scores):
        s = jnp.where(first if j == 0 else band, s, NEG_INF)
        mx = jnp.max(s, axis=-1, keepdims=True)
        probs.append(jnp.exp((s - mx).astype(BF16)))
        maxes.append(mx)
    nh = len(masks)
    one = jnp.ones((), BF16)
    for c0 in range(0, len(chains), nh):
        j, sl, _ = chains[c0]
        vv = keys(vp_ref, vc_ref, j, sl)
        out = None
        lse = None
        for hh, m in enumerate(masks):
            ov = _dot(probs[c0 + hh], jnp.where(m, vv, one))
            den = pltpu.roll(ov, HEAD_DIM, axis=1)
            o = jnp.where(m, ov / den, 0.0)
            l = maxes[c0 + hh] + jnp.log(den)
            out = o if out is None else out + o
            lse = l if lse is None else jnp.where(m, l, lse)
        o_ref[j * Q:(j + 1) * Q, sl] = out.astype(o_ref.dtype)
        lse_ref[j * Q:(j + 1) * Q, sl] = lse


def _attn_call(q, k, v, window, dilation):
    batch, _, L, _ = q.shape
    Q = ATTN_BLOCK
    qb = max(c for c in range(1, ATTN_QB + 1) if L % (c * Q) == 0)
    nb = L // (qb * Q)
    cur = pl.BlockSpec((None, None, qb * Q, ATTN_GW), lambda b, r, n: (b, r, n, 0))
    prev = pl.BlockSpec((None, None, Q, ATTN_GW),
                        lambda b, r, n: (b, r, jnp.maximum(qb * n - 1, 0), 0))
    return pl.pallas_call(
        functools.partial(_attn_kernel, window // dilation),
        grid=(batch, dilation, nb),
        in_specs=[cur, prev, cur, prev, cur],
        out_specs=[cur, cur],
        out_shape=[jax.ShapeDtypeStruct(q.shape, BF16), jax.ShapeDtypeStruct(q.shape, F32)],
        compiler_params=_cparams(("parallel", "parallel", "arbitrary")),
        name=f"dilated_attn_d{dilation}",
    )(q, k, k, v, v)


def _route(logits, bias):
    mxl = jnp.max(logits, axis=0, keepdims=True)
    ex = jnp.exp(logits - mxl)
    probs = ex / jnp.sum(ex, axis=0, keepdims=True)
    sel = probs + bias
    rows = [sel[e:e + 1, :] for e in range(N_EXPERTS)]
    n_groups = N_EXPERTS // EXPERTS_PER_GROUP
    scores = []
    for gi in range(n_groups):
        a, b, c, dd = rows[EXPERTS_PER_GROUP * gi:EXPERTS_PER_GROUP * (gi + 1)]
        hi1, lo1 = jnp.maximum(a, b), jnp.minimum(a, b)
        hi2, lo2 = jnp.maximum(c, dd), jnp.minimum(c, dd)
        scores.append(jnp.maximum(hi1, hi2) + jnp.maximum(jnp.minimum(hi1, hi2), jnp.maximum(lo1, lo2)))
    best_score = scores[0]
    best = jnp.zeros_like(scores[0], dtype=jnp.int32)
    for gi in range(1, n_groups):
        better = scores[gi] > best_score
        best = jnp.where(better, gi, best)
        best_score = jnp.where(better, scores[gi], best_score)
    eid = lax.broadcasted_iota(jnp.int32, sel.shape, 0)
    masked = jnp.where((eid // EXPERTS_PER_GROUP) == best, sel, -jnp.inf)
    m1 = jnp.max(masked, axis=0, keepdims=True)
    i1 = jnp.min(jnp.where(masked == m1, eid, N_EXPERTS), axis=0, keepdims=True)
    masked2 = jnp.where(eid == i1, -jnp.inf, masked)
    m2 = jnp.max(masked2, axis=0, keepdims=True)
    i2 = jnp.min(jnp.where(masked2 == m2, eid, N_EXPERTS), axis=0, keepdims=True)
    p1 = jnp.sum(jnp.where(eid == i1, probs, 0.0), axis=0, keepdims=True)
    p2 = jnp.sum(jnp.where(eid == i2, probs, 0.0), axis=0, keepdims=True)
    tot = p1 + p2
    return jnp.where(eid == i1, p1 / tot, 0.0) + jnp.where(eid == i2, p2 / tot, 0.0)


def _out_kernel(conv_ref, yb_ref, o0_ref, o1_ref, o2_ref, l0_ref, l1_ref, l2_ref, zg_ref, h_ref,
                pa_ref, pb_ref, pc_ref, wo_ref, g_ref, b_ref, wr_ref, rb_ref, h1_ref, comb_ref, cnt_ref, sc_ref):
    d = D_MODEL
    tm = h_ref.shape[0]
    hm = tm // OUT_SPLIT
    nl = ATTN_GW // LANES
    dils = [dil for _, dil in ATTN_GROUPS]
    slot = iter(range(sc_ref.shape[0]))

    def token_order(ref, dil, s):
        n = hm // dil
        if dil == 1:
            return ref[0, s * hm:(s + 1) * hm, :].astype(F32)
        buf = next(slot)
        for res in range(dil):
            blk = ref[res, s * n:(s + 1) * n, :].astype(F32)
            for j in range(nl):
                sc_ref[buf, j, pl.ds(res, n, stride=dil), :] = blk[:, j * LANES:(j + 1) * LANES]
        return jnp.concatenate([sc_ref[buf, j] for j in range(nl)], axis=1)

    subs = range(OUT_SPLIT)
    rows = [slice(s * hm, (s + 1) * hm) for s in subs]
    ya, yb, ycs = [], [], []
    for s, r in enumerate(rows):
        ya.append(_dot(conv_ref[r, :], pa_ref[...]))
        l0, l1, l2 = [token_order(rf, dl, s) for rf, dl in zip((l0_ref, l1_ref, l2_ref), dils)]
        mx = jnp.maximum(jnp.maximum(l0, l1), l2)
        e0, e1, e2 = jnp.exp(l0 - mx), jnp.exp(l1 - mx), jnp.exp(l2 - mx)
        yb.append(_dot(yb_ref[r, :], pb_ref[...]))
        yc = e0 * token_order(o0_ref, dils[0], s)
        yc = yc + e1 * token_order(o1_ref, dils[1], s)
        yc = (yc + e2 * token_order(o2_ref, dils[2], s)) / (e0 + e1 + e2)
        ycs.append(yc.astype(BF16))
    yc_full = _dot(jnp.concatenate(ycs, axis=0), pc_ref[...])
    yc = [yc_full[r, :] for r in rows]
    mixes = []
    for s, r in enumerate(rows):
        ga = zg_ref[r, 0:d].astype(F32)
        gb = zg_ref[r, d:2 * d].astype(F32)
        gc = zg_ref[r, 2 * d:3 * d].astype(F32)
        mixes.append((ga * ya[s] + gb * yb[s] + gc * yc[s]).astype(BF16))
    outs = [_dot(m, wo_ref[...]) for m in mixes]
    wr = wr_ref[...]
    wr_hi = wr.astype(BF16)
    wr_hl = jnp.concatenate([wr_hi, (wr - wr_hi.astype(F32)).astype(BF16)], axis=0)
    logits = []
    for s, r in enumerate(rows):
        h1 = _layer_norm(ALPHA * h_ref[r, :] + outs[s], g_ref[...], b_ref[...])
        h1_ref[r, :] = h1
        h_hi = h1.astype(BF16)
        t1 = _dot_nt(wr_hl, h_hi)
        t2 = _dot_nt(wr_hi, (h1 - h_hi.astype(F32)).astype(BF16))
        logits.append(t1[:N_EXPERTS] + (t1[N_EXPERTS:] + t2))
    logits = jnp.concatenate(logits, axis=1)
    comb = _route(logits, rb_ref[...])
    comb_ref[...] = comb
    count = jnp.sum((comb > 0.0).astype(F32), axis=1, keepdims=True)
    cnt_ref[...] = jnp.broadcast_to(count, cnt_ref.shape)


def _out_call(conv, yb, outs, lses, z, h, lp, w_router_t, router_bias, seq, tm=512):
    t, d = h.shape
    tps = seq // tm
    row = lambda w, j=0: pl.BlockSpec((tm, w), lambda i, j=j: (i, j))
    full = lambda a: pl.BlockSpec(a.shape, lambda i: (0,) * a.ndim)
    res = [pl.BlockSpec((None, dil, tm // dil, ATTN_GW), lambda i: (i // tps, 0, i % tps, 0))
           for _, dil in ATTN_GROUPS]
    consts = [lp["p_a"], lp["p_b"], lp["p_c"], lp["w_o"], lp["ln1_g"], lp["ln1_b"], w_router_t, router_bias]
    n_relayout = 2 * OUT_SPLIT * sum(1 for _, dil in ATTN_GROUPS if dil > 1)
    return pl.pallas_call(
        _out_kernel,
        grid=(t // tm,),
        in_specs=[row(d), row(d)] + res + res + [row(3 * d), row(d)]
                 + [full(a) for a in consts],
        out_specs=[row(d), pl.BlockSpec((N_EXPERTS, tm), lambda i: (0, i)),
                   pl.BlockSpec((None, N_EXPERTS, LANES), lambda i: (i, 0, 0))],
        out_shape=[jax.ShapeDtypeStruct((t, d), F32), jax.ShapeDtypeStruct((N_EXPERTS, t), F32),
                   jax.ShapeDtypeStruct((t // tm, N_EXPERTS, LANES), F32)],
        scratch_shapes=[pltpu.VMEM((n_relayout, ATTN_GW // LANES, tm // OUT_SPLIT, LANES), F32)],
        compiler_params=_cparams(("parallel",)),
        name="mixer_out",
    )(conv, yb, *outs, *lses, z, h, *consts)


def _moe_kernel(cnt_ref, x_ref, comb_ref, wg_ref, wu_ref, wd_ref, g_ref, b_ref, o_ref,
                xb_ref, acc_ref, rank_ref):
    i = pl.program_id(0)
    e = pl.program_id(1)
    tm = x_ref.shape[0]
    R = MOE_ROWS

    @pl.when(e == 0)
    def _():
        xb_ref[...] = x_ref[...].astype(BF16)
        acc_ref[...] = jnp.zeros_like(acc_ref)
        sel = (comb_ref[...] > 0.0).astype(BF16)
        tri = (lax.broadcasted_iota(jnp.int32, (tm, tm), 0)
               <= lax.broadcasted_iota(jnp.int32, (tm, tm), 1)).astype(BF16)
        rank_ref[...] = _dot(sel, tri) - 1.0

    comb_e = comb_ref[pl.ds(e, 1), :]
    rank_e = jnp.where(comb_e > 0.0, rank_ref[pl.ds(e, 1), :], -1.0)
    slot = lax.broadcasted_iota(jnp.int32, (R, tm), 0).astype(F32)

    def block(blk, carry):
        hit = rank_e == slot + (blk * R).astype(F32)
        onehot = hit.astype(BF16)
        xg = _dot(onehot, xb_ref[...]).astype(BF16)
        cw = jnp.sum(jnp.where(hit, comb_e, 0.0), axis=1, keepdims=True)
        gate = _dot(xg, wg_ref[...])
        up = _dot(xg, wu_ref[...])
        hid = gate * _sigmoid(gate) * up * cw
        y = _dot(hid.astype(BF16), wd_ref[...]).astype(BF16)
        acc_ref[...] += _dot_tn(onehot, y)
        return carry

    lax.fori_loop(0, (cnt_ref[i, e] + R - 1) // R, block, 0)

    @pl.when(e == N_EXPERTS - 1)
    def _():
        o_ref[...] = _layer_norm(ALPHA * x_ref[...] + acc_ref[...], g_ref[...], b_ref[...])


def _moe_call(h1, comb_t, counts, lp, tm=1024):
    t, d = h1.shape
    wsp = lambda s: pl.BlockSpec((None,) + s, lambda i, e, c: (e, 0, 0))
    vec = pl.BlockSpec((1, d), lambda i, e, c: (0, 0))
    return pl.pallas_call(
        _moe_kernel,
        grid_spec=pltpu.PrefetchScalarGridSpec(
            num_scalar_prefetch=1,
            grid=(t // tm, N_EXPERTS),
            in_specs=[pl.BlockSpec((tm, d), lambda i, e, c: (i, 0)),
                      pl.BlockSpec((N_EXPERTS, tm), lambda i, e, c: (0, i)),
                      wsp((d, D_EXPERT)), wsp((d, D_EXPERT)), wsp((D_EXPERT, d)), vec, vec],
            out_specs=pl.BlockSpec((tm, d), lambda i, e, c: (i, 0)),
            scratch_shapes=[pltpu.VMEM((tm, d), BF16), pltpu.VMEM((tm, d), F32),
                            pltpu.VMEM((N_EXPERTS, tm), F32)]),
        out_shape=jax.ShapeDtypeStruct((t, d), F32),
        compiler_params=_cparams(("parallel", "arbitrary")),
        name="moe",
    )(counts, h1, comb_t, lp["w_gate"], lp["w_up"], lp["w_down"], lp["ln2_g"], lp["ln2_b"])


def _pad_rows(w, start, total):
    return jnp.zeros((total, w.shape[1]), w.dtype).at[start:start + w.shape[0]].set(w)


def _rotary_tables(positions):
    half = ROT_DIM // 2
    inv_freq = ROPE_THETA ** (-jnp.arange(0, ROT_DIM, 2, dtype=F32) / ROT_DIM)
    ang = positions.astype(F32).reshape(-1, 1) * inv_freq
    cos, sin = jnp.cos(ang), jnp.sin(ang)
    t = ang.shape[0]
    pad = HEAD_DIM - ROT_DIM
    rc = jnp.concatenate([cos, cos, jnp.ones((t, pad), F32)], axis=1)
    rm = jnp.concatenate([-sin, jnp.zeros((t, half + pad), F32)], axis=1)
    rp = jnp.concatenate([jnp.zeros((t, half), F32), sin, jnp.zeros((t, pad), F32)], axis=1)
    return tuple(jnp.tile(x, (1, HEADS_PER_VREG)) for x in (rc, rm, rp))


def _pack_in_proj(w):
    d = D_MODEL
    n_lora = DECAY_LORA + AAA_LORA + GATE_LORA
    gate, conv, rkv = w[:, :3 * d], w[:, 3 * d:6 * d], w[:, 6 * d:9 * d]
    lora = jnp.pad(w[:, 9 * d:9 * d + n_lora], ((0, 0), (0, LORA_PAD - n_lora)))
    attn = w[:, 9 * d + n_lora:]
    q, k, v = attn[:, :ATTN_W], attn[:, ATTN_W:2 * ATTN_W], attn[:, 2 * ATTN_W:]
    vg = [v[:, gi * ATTN_GW:(gi + 1) * ATTN_GW] for gi in range(len(ATTN_GROUPS))]
    cat = lambda parts: jnp.concatenate(parts, axis=1).astype(BF16)
    return {"w_pg": cat([gate]), "w_pc": cat([conv, vg[0]]), "w_pr": cat([rkv, lora, vg[1]]),
            "w_pa": cat([q, k, vg[2]])}


def kernel(x, positions, ln_in_g, ln_in_b, w_in, conv_w, rwkv_mu, rwkv_w0, rwkv_w2, rwkv_a0, rwkv_a2, rwkv_g2, rwkv_v0, rwkv_v1, rwkv_v2, rwkv_k_k, rwkv_k_a, rwkv_r_k, rwkv_lnx_g, rwkv_lnx_b, p_a, p_b, p_c, w_o, ln1_g, ln1_b, w_router, router_bias, w_gate, w_up, w_down, ln2_g, ln2_b):
    batch, seq, d = x.shape
    t = batch * seq
    depth = w_in.shape[0]
    assert d == D_MODEL and seq % (ATTN_BLOCK * ATTN_GROUPS[-1][1]) == 0
    rot_tabs = _rotary_tables(positions)
    hid = lax.broadcasted_iota(jnp.int32, (LANES, LANES), 0) // HEAD_DIM
    seg = (hid == hid.T).astype(BF16)
    w_router_t = w_router.T.astype(F32)
    rbias = router_bias.reshape(N_EXPERTS, 1).astype(F32)
    vec = lambda a: a.reshape(1, -1).astype(F32)

    h = x.reshape(t, d)
    v_first = None
    for l in range(depth):
        lp = {
            "conv_w": conv_w[l],
            "mu_r": vec(rwkv_mu[l][:3 * d]),
            "mu_l": vec(jnp.pad(rwkv_mu[l][3 * d:], (0, LORA_PAD - 288))),
            "w0": vec(rwkv_w0[l]),
            "w2": _pad_rows(rwkv_w2[l], 0, LANES).astype(BF16),
            "a0": vec(rwkv_a0[l]),
            "a2": _pad_rows(rwkv_a2[l], DECAY_LORA, LANES).astype(BF16),
            "g2": _pad_rows(rwkv_g2[l], 0, LORA_PAD - LANES).astype(BF16),
            "k_k": vec(rwkv_k_k[l]), "k_a": vec(rwkv_k_a[l]), "r_k": vec(rwkv_r_k[l]),
            "seg": seg,
            "p_a": p_a[l].astype(BF16), "p_b": p_b[l].astype(BF16), "p_c": p_c[l].astype(BF16),
            "w_o": w_o[l].astype(BF16), "ln1_g": vec(ln1_g[l]), "ln1_b": vec(ln1_b[l]),
            "w_gate": w_gate[l].astype(BF16), "w_up": w_up[l].astype(BF16), "w_down": w_down[l].astype(BF16),
            "ln2_g": vec(ln2_g[l]), "ln2_b": vec(ln2_b[l]),
        }
        if l > 0:
            lp["v0"] = vec(rwkv_v0[l - 1])
            lp["v1"] = jnp.pad(rwkv_v1[l - 1], ((0, 0), (0, MV_PAD - MV_LORA))).astype(BF16)
            lp["v2"] = _pad_rows(rwkv_v2[l - 1], 0, MV_PAD).astype(BF16)
        lp.update(_pack_in_proj(w_in[l]))
        h, gates, conv, q_g, k_g, v_g0, v_g2 = _in_proj(
            h, lp, rot_tabs, batch, seq, ln=(ln_in_g, ln_in_b) if l == 0 else None)
        prep = _prep_call(h, lp, v_first, ATTN_GROUPS[1][1], batch, seq)
        r, lw, k, v, a, b, g, bonus = prep[:8]
        v_g = [v_g0, prep[-1], v_g2]
        if l == 0:
            v_first = prep[8]
        yb = _scan_call(r, lw, k, v, a, b, bonus, g, vec(rwkv_lnx_g[l]), vec(rwkv_lnx_b[l]), batch, seq)
        outs, lses = [], []
        for gi, (window, dilation) in enumerate(ATTN_GROUPS):
            o, s = _attn_call(q_g[gi], k_g[gi], v_g[gi], window, dilation)
            outs.append(o)
            lses.append(s)
        h1, comb_t, cnt = _out_call(conv, yb, outs, lses, gates, h, lp, w_router_t, rbias, seq)
        counts = cnt[:, :, 0].reshape(t // MOE_TM, -1, N_EXPERTS).sum(axis=1).astype(jnp.int32)
        h = _moe_call(h1, comb_t, counts, lp, MOE_TM)
    return h.reshape(batch, seq, d)
```
